```python
import jax, jax.numpy as jnp
from jax import lax
import numpy as np

D_MODEL = 2048
BATCH = 1
SEQ = 8192
DEPTH = 1
DEC_BATCH = 128
DEC_SEQ = 1
PAST_LEN = 2048
PAGE_SIZE = 128

HEAD_DIM = 128
H_A = 8
H_B = 8
H_IDX = 16
D_IDX = 64
TOPK_MAX = 256
ROT_A = HEAD_DIM // 4
ROT_IDX = D_IDX // 4
ROPE_THETA = 500000.0
Q_BLOCK = 128
N_GROUPS = 4
EXPERTS_PER_GROUP = 8
N_EXPERTS = N_GROUPS * EXPERTS_PER_GROUP
TOP_K_IN_GROUP = 2
D_EXPERT = 512
EPS = 1e-6
W_A = H_A * HEAD_DIM
W_B = H_B * HEAD_DIM
PROJ_SIZES = (W_A, W_A, W_A, H_IDX * D_IDX, D_IDX, H_IDX, W_B, W_B, W_B, H_B, D_MODEL, D_MODEL)
D_PROJ = 3 * W_A + H_IDX * D_IDX + D_IDX + H_IDX + 3 * W_B + H_B + 2 * D_MODEL

kernel_name = "hybrid_dsa_fox_hmoe_step"

F32 = jnp.float32


def rmsnorm(x, g):
    xf = x.astype(F32)
    y = xf * lax.rsqrt(jnp.mean(xf * xf, axis=-1, keepdims=True) + EPS)
    return (y * g.astype(F32)).astype(x.dtype)


def rope_partial(x, pos, rot_dim):
    half = rot_dim // 2
    inv_freq = ROPE_THETA ** (-jnp.arange(half, dtype=F32) / half)
    ang = pos.astype(F32)[:, None] * inv_freq[None, :]
    cos = jnp.cos(ang)[None, :, None, :]
    sin = jnp.sin(ang)[None, :, None, :]
    xr = x[..., :rot_dim].astype(F32)
    x1, x2 = xr[..., :half], xr[..., half:]
    rot = jnp.concatenate([x1 * cos - x2 * sin, x2 * cos + x1 * sin], axis=-1).astype(x.dtype)
    return jnp.concatenate([rot, x[..., rot_dim:]], axis=-1)


def project_inputs(xn, w_in, f_bias, pos):
    n, t, _ = xn.shape
    p = jnp.einsum('ntd,de->nte', xn, w_in)
    bounds = np.cumsum(PROJ_SIZES)[:-1].tolist()
    q_a, k_a, v_a, q_i, k_i, w_i, q_b, k_b, v_b, f_b, g_a, g_b = jnp.split(p, bounds, axis=-1)
    q_a = rope_partial(q_a.reshape(n, t, H_A, HEAD_DIM), pos, ROT_A)
    k_a = rope_partial(k_a.reshape(n, t, H_A, HEAD_DIM), pos, ROT_A)
    v_a = v_a.reshape(n, t, H_A, HEAD_DIM)
    q_i = rope_partial(q_i.reshape(n, t, H_IDX, D_IDX), pos, ROT_IDX)
    k_i = rope_partial(k_i[:, :, None, :], pos, ROT_IDX)[:, :, 0, :]
    w_i = w_i * (H_IDX ** -0.5)
    q_b = q_b.reshape(n, t, H_B, HEAD_DIM)
    k_b = k_b.reshape(n, t, H_B, HEAD_DIM)
    v_b = v_b.reshape(n, t, H_B, HEAD_DIM)
    log_f = jax.nn.log_sigmoid((f_b + f_bias).astype(F32))
    return (q_a, k_a, v_a, q_i, k_i, w_i, q_b, k_b, v_b, log_f,
            jax.nn.sigmoid(g_a), jax.nn.sigmoid(g_b))


def indexer_scores(q_i, w_i, k_i):
    s = jnp.einsum('nqhd,nld->nqhl', q_i.astype(F32), k_i.astype(F32)) * (D_IDX ** -0.5)
    return jnp.einsum('nqhl,nqh->nql', jax.nn.relu(s), w_i.astype(F32))


def attend_selected(q, k_sel, v_sel, valid):
    s = jnp.einsum('nqhd,nqkhd->nqhk', q, k_sel).astype(F32) * (HEAD_DIM ** -0.5)
    s = jnp.where(valid[:, :, None, :], s, -jnp.inf)
    p = jax.nn.softmax(s, axis=-1)
    return jnp.einsum('nqhk,nqkhd->nqhd', p.astype(v_sel.dtype), v_sel)


take_rows = jax.vmap(lambda a, i: a[i])


def dsa_prompt(q, k, v, q_i, k_i, w_i):
    b, s_len = q.shape[0], q.shape[1]
    k_sel_n = min(TOPK_MAX, s_len // 4)
    kpos = jnp.arange(s_len)

    def block(j):
        s0 = j * Q_BLOCK
        qb = lax.dynamic_slice_in_dim(q, s0, Q_BLOCK, axis=1)
        qib = lax.dynamic_slice_in_dim(q_i, s0, Q_BLOCK, axis=1)
        wib = lax.dynamic_slice_in_dim(w_i, s0, Q_BLOCK, axis=1)
        qpos = s0 + jnp.arange(Q_BLOCK)
        score = indexer_scores(qib, wib, k_i)
        score = jnp.where(kpos[None, None, :] <= qpos[None, :, None], score, -jnp.inf)
        _, sel = lax.top_k(score, k_sel_n)
        valid = sel <= qpos[None, :, None]
        return attend_selected(qb, take_rows(k, sel), take_rows(v, sel), valid)

    out = lax.map(block, jnp.arange(s_len // Q_BLOCK))
    return out.transpose(1, 0, 2, 3, 4).reshape(b, s_len, H_A, HEAD_DIM)


def fox_prompt(q, k, v, log_f):
    b, s_len = q.shape[0], q.shape[1]
    c = jnp.cumsum(log_f, axis=1).transpose(0, 2, 1)
    kpos = jnp.arange(s_len)

    def block(j):
        s0 = j * Q_BLOCK
        qb = lax.dynamic_slice_in_dim(q, s0, Q_BLOCK, axis=1)
        cq = lax.dynamic_slice_in_dim(c, s0, Q_BLOCK, axis=2)
        qpos = s0 + jnp.arange(Q_BLOCK)
        sc = (jnp.einsum('bqhd,bshd->bhqs', qb, k).astype(F32) * (HEAD_DIM ** -0.5)
              + (cq[..., :, None] - c[:, :, None, :]))
        sc = jnp.where(kpos[None, None, None, :] <= qpos[None, None, :, None], sc, -jnp.inf)
        p = jax.nn.softmax(sc, axis=-1)
        return jnp.einsum('bhqs,bshd->bqhd', p.astype(v.dtype), v)

    out = lax.map(block, jnp.arange(s_len // Q_BLOCK))
    return out.transpose(1, 0, 2, 3, 4).reshape(b, s_len, H_B, HEAD_DIM)


def gather_pages(cache, page_table):
    g = cache[page_table]
    return g.reshape((page_table.shape[0], -1) + cache.shape[2:])


def dsa_sample(q, k_new, v_new, q_i, k_i_new, w_i, cache_k, cache_v, cache_kidx, page_table):
    n, t = q.shape[0], q.shape[1]
    L = PAST_LEN + t
    k_sel_n = min(TOPK_MAX, L // 4)
    ki_all = jnp.concatenate([gather_pages(cache_kidx, page_table).astype(F32), k_i_new.astype(F32)], axis=1)
    qpos = PAST_LEN + jnp.arange(t)
    kpos = jnp.arange(L)
    score = indexer_scores(q_i, w_i, ki_all)
    score = jnp.where(kpos[None, None, :] <= qpos[None, :, None], score, -jnp.inf)
    _, sel = lax.top_k(score, k_sel_n)
    is_past = sel < PAST_LEN
    s_past = jnp.minimum(sel, PAST_LEN - 1)
    phys = page_table[jnp.arange(n)[:, None, None], s_past // PAGE_SIZE]
    off = s_past % PAGE_SIZE
    s_new = jnp.clip(sel - PAST_LEN, 0, t - 1)
    k_sel = jnp.where(is_past[..., None, None], cache_k[phys, off], take_rows(k_new, s_new))
    v_sel = jnp.where(is_past[..., None, None], cache_v[phys, off], take_rows(v_new, s_new))
    valid = sel <= qpos[None, :, None]
    return attend_selected(q, k_sel, v_sel, valid)


def fox_sample(q, k_new, v_new, log_f, cache_k, cache_v, cache_logf, page_table):
    t = q.shape[1]
    L = PAST_LEN + t
    kp = gather_pages(cache_k, page_table)
    vp = gather_pages(cache_v, page_table)
    lp = gather_pages(cache_logf, page_table).astype(F32)
    c = jnp.cumsum(jnp.concatenate([lp, log_f], axis=1), axis=1).transpose(0, 2, 1)
    cq = c[:, :, PAST_LEN:]
    qpos = PAST_LEN + jnp.arange(t)
    kpos = jnp.arange(L)
    sc = jnp.concatenate([jnp.einsum('nqhd,nshd->nhqs', q, kp),
                          jnp.einsum('nqhd,nshd->nhqs', q, k_new)], axis=-1).astype(F32)
    sc = sc * (HEAD_DIM ** -0.5) + (cq[..., :, None] - c[:, :, None, :])
    sc = jnp.where(kpos[None, None, None, :] <= qpos[None, None, :, None], sc, -jnp.inf)
    p = jax.nn.softmax(sc, axis=-1)
    return (jnp.einsum('nhqs,nshd->nqhd', p[..., :PAST_LEN].astype(vp.dtype), vp)
            + jnp.einsum('nhqs,nshd->nqhd', p[..., PAST_LEN:].astype(v_new.dtype), v_new))


def hier_moe(x, w_grp, w_exp, w_up1, w_up3, w_down):
    t = x.shape[0]
    grp_logits = jnp.einsum('td,dg->tg', x, w_grp).astype(F32)
    grp_p = jax.nn.softmax(grp_logits, axis=-1)
    g_idx = jnp.argmax(grp_logits, axis=-1)
    g_w = jnp.take_along_axis(grp_p, g_idx[:, None], axis=1)
    exp_logits = jnp.einsum('td,de->te', x, w_exp).astype(F32).reshape(t, N_GROUPS, EXPERTS_PER_GROUP)
    in_logits = jnp.take_along_axis(exp_logits, g_idx[:, None, None], axis=1)[:, 0]
    in_p = jax.nn.softmax(in_logits, axis=-1)
    top_w, top_i = lax.top_k(in_p, TOP_K_IN_GROUP)
    top_w = top_w / jnp.sum(top_w, axis=-1, keepdims=True)
    eids = g_idx[:, None] * EXPERTS_PER_GROUP + top_i
    gate = jnp.zeros((t, N_EXPERTS), F32).at[jnp.arange(t)[:, None], eids].add(top_w * g_w)
    h = jax.nn.silu(jnp.einsum('td,edf->tef', x, w_up1)) * jnp.einsum('td,edf->tef', x, w_up3)
    h = h * gate[..., None].astype(h.dtype)
    return jnp.einsum('tef,efd->td', h, w_down)


def merge_and_ffn(x, o_a, o_b, gate_a, gate_b, w_a_o, w_b_o, w_out, g_ffn, w_grp, w_exp, w_up1, w_up3, w_down):
    n, t, _ = x.shape
    br_a = jnp.einsum('nte,ed->ntd', o_a.reshape(n, t, W_A), w_a_o)
    br_b = jnp.einsum('nte,ed->ntd', o_b.reshape(n, t, W_B), w_b_o)
    h = x + jnp.einsum('ntd,de->nte', gate_a * br_a + gate_b * br_b, w_out)
    hn = rmsnorm(h, g_ffn).reshape(n * t, D_MODEL)
    return h + hier_moe(hn, w_grp, w_exp, w_up1, w_up3, w_down).reshape(n, t, D_MODEL)


def setup_inputs(seed: int = 0) -> dict:
    key = jax.random.key(seed)
    ks = jax.random.split(key, 24)
    n_pages = PAST_LEN // PAGE_SIZE
    n_used = DEC_BATCH * n_pages
    n_pool = n_used + max(1, n_used // 4)

    def nrm(k, shape, scale=1.0):
        return scale * jax.random.normal(k, shape, F32)

    page_table = jax.random.permutation(ks[8], n_pool)[:n_used].reshape(DEC_BATCH, n_pages).astype(jnp.int32)
    return {
        "x_prompt": nrm(ks[0], (BATCH, SEQ, D_MODEL)),
        "x_sample": nrm(ks[1], (DEC_BATCH, DEC_SEQ, D_MODEL)),
        "cache_k_a": nrm(ks[2], (DEPTH, n_pool, PAGE_SIZE, H_A, HEAD_DIM)),
        "cache_v_a": nrm(ks[3], (DEPTH, n_pool, PAGE_SIZE, H_A, HEAD_DIM)),
        "cache_kidx_a": nrm(ks[4], (DEPTH, n_pool, PAGE_SIZE, D_IDX)),
        "cache_k_b": nrm(ks[5], (DEPTH, n_pool, PAGE_SIZE, H_B, HEAD_DIM)),
        "cache_v_b": nrm(ks[6], (DEPTH, n_pool, PAGE_SIZE, H_B, HEAD_DIM)),
        "cache_logf_b": jax.nn.log_sigmoid(2.0 + nrm(ks[7], (DEPTH, n_pool, PAGE_SIZE, H_B), 0.7)),
        "page_table": page_table,
        "g_mix": 1.0 + nrm(ks[9], (DEPTH, D_MODEL), 0.02),
        "w_in": nrm(ks[10], (DEPTH, D_MODEL, D_PROJ), D_MODEL ** -0.5),
        "f_bias": 2.0 + nrm(ks[11], (DEPTH, H_B), 0.1),
        "w_a_o": nrm(ks[12], (DEPTH, W_A, D_MODEL), W_A ** -0.5),
        "w_b_o": nrm(ks[13], (DEPTH, W_B, D_MODEL), W_B ** -0.5),
        "w_out": nrm(ks[14], (DEPTH, D_MODEL, D_MODEL), D_MODEL ** -0.5),
        "g_ffn": 1.0 + nrm(ks[15], (DEPTH, D_MODEL), 0.02),
        "w_grp": nrm(ks[16], (DEPTH, D_MODEL, N_GROUPS), D_MODEL ** -0.5),
        "w_exp": nrm(ks[17], (DEPTH, D_MODEL, N_EXPERTS), D_MODEL ** -0.5),
        "w_up1": nrm(ks[18], (DEPTH, N_EXPERTS, D_MODEL, D_EXPERT), D_MODEL ** -0.5),
        "w_up3": nrm(ks[19], (DEPTH, N_EXPERTS, D_MODEL, D_EXPERT), D_MODEL ** -0.5),
        "w_down": nrm(ks[20], (DEPTH, N_EXPERTS, D_EXPERT, D_MODEL), D_EXPERT ** -0.5),
        "g_final": 1.0 + nrm(ks[21], (D_MODEL,), 0.02),
    }


def reference(x_prompt, x_sample, cache_k_a, cache_v_a, cache_kidx_a, cache_k_b, cache_v_b, cache_logf_b,
              page_table, g_mix, w_in, f_bias, w_a_o, w_b_o, w_out, g_ffn, w_grp, w_exp, w_up1, w_up3,
              w_down, g_final):
    pos_p = jnp.arange(SEQ)
    pos_s = PAST_LEN + jnp.arange(DEC_SEQ)
    hp, hs = x_prompt, x_sample
    rows_p, rows_s = [], []
    for l in range(DEPTH):
        ffn_w = (w_a_o[l], w_b_o[l], w_out[l], g_ffn[l], w_grp[l], w_exp[l], w_up1[l], w_up3[l], w_down[l])
        qa, ka, va, qi, ki, wi, qb, kb, vb, lf, ga, gb = project_inputs(rmsnorm(hp, g_mix[l]), w_in[l], f_bias[l], pos_p)
        oa = dsa_prompt(qa, ka, va, qi, ki, wi)
        ob = fox_prompt(qb, kb, vb, lf)
        hp = merge_and_ffn(hp, oa, ob, ga, gb, *ffn_w)
        rows_p.append((ka, va, ki, kb, vb, lf))
        qa, ka, va, qi, ki, wi, qb, kb, vb, lf, ga, gb = project_inputs(rmsnorm(hs, g_mix[l]), w_in[l], f_bias[l], pos_s)
        oa = dsa_sample(qa, ka, va, qi, ki, wi, cache_k_a[l], cache_v_a[l], cache_kidx_a[l], page_table)
        ob = fox_sample(qb, kb, vb, lf, cache_k_b[l], cache_v_b[l], cache_logf_b[l], page_table)
        hs = merge_and_ffn(hs, oa, ob, ga, gb, *ffn_w)
        rows_s.append((ka, va, ki, kb, vb, lf))
    y_prompt = rmsnorm(hp, g_final)
    y_sample = rmsnorm(hs, g_final)
    k_a_p = jnp.stack([r[0] for r in rows_p])
    v_a_p = jnp.stack([r[1] for r in rows_p])
    kidx_p = jnp.stack([r[2] for r in rows_p])
    k_b_p = jnp.stack([r[3] for r in rows_p])
    v_b_p = jnp.stack([r[4] for r in rows_p])
    logf_p = jnp.stack([r[5] for r in rows_p])
    k_a_s = jnp.stack([r[0] for r in rows_s])
    v_a_s = jnp.stack([r[1] for r in rows_s])
    kidx_s = jnp.stack([r[2] for r in rows_s])
    k_b_s = jnp.stack([r[3] for r in rows_s])
    v_b_s = jnp.stack([r[4] for r in rows_s])
    logf_s = jnp.stack([r[5] for r in rows_s])
    return (y_prompt, y_sample, k_a_p, v_a_p, kidx_p, k_b_p, v_b_p, logf_p,
            k_a_s, v_a_s, kidx_s, k_b_s, v_b_s, logf_s)
```

```python
import functools

import numpy as np
import jax
import jax.numpy as jnp
from jax import lax
from jax.experimental import pallas as pl
from jax.experimental.pallas import tpu as pltpu

F32 = jnp.float32
BF16 = jnp.bfloat16
I32 = jnp.int32

HEAD_DIM = 128
H_IDX = 16
D_IDX = 64
TOPK_MAX = 256
ROT_A = HEAD_DIM // 4
ROT_IDX = D_IDX // 4
ROPE_THETA = 500000.0
N_GROUPS = 4
EXPERTS_PER_GROUP = 8
N_EXPERTS = N_GROUPS * EXPERTS_PER_GROUP
EPS = 1e-6

LANES = 128
NEG = -1e30
INT_MIN = -(2 ** 31)
VMEM_LIMIT = 56 * 1024 * 1024

NT_DIMS = (((1,), (1,)), ((), ()))


def _cparams(*sem, vmem=VMEM_LIMIT):
    return pltpu.CompilerParams(dimension_semantics=sem, vmem_limit_bytes=vmem)


def _rmsnorm_kernel(x_ref, g_ref, o_ref):
    x = x_ref[...]
    y = x * lax.rsqrt(jnp.mean(x * x, axis=-1, keepdims=True) + EPS)
    o_ref[...] = (y * g_ref[...]).astype(o_ref.dtype)


def _rmsnorm(x, g, out_dtype, tm):
    m, d = x.shape
    return pl.pallas_call(
        _rmsnorm_kernel,
        grid=(m // tm,),
        in_specs=[pl.BlockSpec((tm, d), lambda i: (i, 0)), pl.BlockSpec((1, d), lambda i: (0, 0))],
        out_specs=pl.BlockSpec((tm, d), lambda i: (i, 0)),
        out_shape=jax.ShapeDtypeStruct((m, d), out_dtype),
        compiler_params=_cparams("parallel"),
        name="rmsnorm",
    )(x, g.reshape(1, d))


def _rope_tables(pos, rot_dim, period, active_lanes):
    half = rot_dim // 2
    inv_freq = ROPE_THETA ** (-jnp.arange(half, dtype=F32) / half)
    ang = pos.astype(F32)[:, None] * inv_freq[None, :]
    cos, sin = jnp.cos(ang), jnp.sin(ang)
    lane = np.arange(LANES)
    d = lane % period
    first = (d < half) & (lane < active_lanes)
    second = (d >= half) & (d < rot_dim) & (lane < active_lanes)
    idx = np.where(first, d, np.where(second, d - half, 0))
    cos_l, sin_l = cos[:, idx], sin[:, idx]
    rot = jnp.asarray(first | second)[None, :]
    c = jnp.where(rot, cos_l, 1.0)
    s1 = jnp.where(jnp.asarray(first)[None, :], -sin_l, 0.0)
    s2 = jnp.where(jnp.asarray(second)[None, :], sin_l, 0.0)
    return c, s1, s2


def _rope_lanes(y, c, s1, s2, half):
    return y * c + pltpu.roll(y, LANES - half, 1) * s1 + pltpu.roll(y, half, 1) * s2


def _proj_kernel(*refs, mode, half, n_out):
    x_ref, w_ref = refs[0], refs[1]
    outs = refs[len(refs) - n_out:]
    acc = jnp.dot(x_ref[...], w_ref[...], preferred_element_type=F32)
    tn = acc.shape[1]
    if mode == "rope":
        c, s1, s2 = refs[2][...], refs[3][...], refs[4][...]
        for b in range(tn // LANES):
            sl = slice(b * LANES, (b + 1) * LANES)
            y = _rope_lanes(acc[:, sl], c, s1, s2, half)
            for o in outs:
                o[:, sl] = y.astype(o.dtype)
        return
    if mode == "sigmoid":
        acc = 1.0 / (1.0 + jnp.exp(-acc))
    for o in outs:
        o[...] = acc.astype(o.dtype)


def _proj(xn, w, out_dtypes, mode="none", tables=None, half=0, tm=512, tn=1024):
    m, d = xn.shape
    n = w.shape[1]
    tm = min(tm, m)
    in_specs = [pl.BlockSpec((tm, d), lambda j, i: (i, 0)), pl.BlockSpec((d, tn), lambda j, i: (0, j))]
    args = [xn, w]
    if mode == "rope":
        in_specs += [pl.BlockSpec((tm, LANES), lambda j, i: (i, 0))] * 3
        args += list(tables)
    outs = pl.pallas_call(
        functools.partial(_proj_kernel, mode=mode, half=half, n_out=len(out_dtypes)),
        grid=(n // tn, m // tm),
        in_specs=in_specs,
        out_specs=[pl.BlockSpec((tm, tn), lambda j, i: (i, j)) for _ in out_dtypes],
        out_shape=[jax.ShapeDtypeStruct((m, n), dt) for dt in out_dtypes],
        compiler_params=_cparams("parallel", "parallel"),
        name="proj_" + mode,
    )(*args)
    return outs


def _proj_small_kernel(x_ref, w_ref, c_ref, s1_ref, s2_ref, bias_ref, o_ref, *, half, wi_scale):
    acc = jnp.dot(x_ref[...], w_ref[...], preferred_element_type=F32)
    roped = _rope_lanes(acc, c_ref[...], s1_ref[...], s2_ref[...], half)
    z = acc + bias_ref[...]
    logsig = -(jnp.maximum(-z, 0.0) + jnp.log(1.0 + jnp.exp(-jnp.abs(z))))
    lane = lax.broadcasted_iota(I32, acc.shape, 1)
    o_ref[...] = jnp.where(lane < D_IDX, roped, jnp.where(lane < D_IDX + H_IDX, acc * wi_scale, logsig))


def _proj_small(xn, w_small, tables, bias_row, tm=512):
    m, d = xn.shape
    tm = min(tm, m)
    wi_scale = (H_IDX ** -0.5) * (D_IDX ** -0.5)
    return pl.pallas_call(
        functools.partial(_proj_small_kernel, half=ROT_IDX // 2, wi_scale=wi_scale),
        grid=(m // tm,),
        in_specs=[pl.BlockSpec((tm, d), lambda i: (i, 0)), pl.BlockSpec((d, LANES), lambda i: (0, 0))]
        + [pl.BlockSpec((tm, LANES), lambda i: (i, 0))] * 3 + [pl.BlockSpec((1, LANES), lambda i: (0, 0))],
        out_specs=pl.BlockSpec((tm, LANES), lambda i: (i, 0)),
        out_shape=jax.ShapeDtypeStruct((m, LANES), F32),
        compiler_params=_cparams("parallel"),
        name="proj_small",
    )(xn, w_small, *tables, bias_row)


def _project_group(x2d, pos, g_mix, w_in, f_bias):
    m, d = x2d.shape
    h_a = h_b = (w_in.shape[1] - (H_IDX * D_IDX + D_IDX + H_IDX) - 2 * d) // (6 * HEAD_DIM + 1)
    w_a = h_a * HEAD_DIM
    sizes = (w_a, w_a, w_a, H_IDX * D_IDX, D_IDX, H_IDX, w_a, w_a, w_a, h_b, d, d)
    offs = np.concatenate([[0], np.cumsum(sizes)])
    seg = lambda k: w_in[:, offs[k]:offs[k + 1]].astype(BF16)

    xn = _rmsnorm(x2d, g_mix, BF16, tm=min(512, m))
    tab_a = _rope_tables(pos, ROT_A, HEAD_DIM, LANES)
    tab_i = _rope_tables(pos, ROT_IDX, D_IDX, LANES)
    tab_k = _rope_tables(pos, ROT_IDX, D_IDX, ROT_IDX)

    (q_a,) = _proj(xn, seg(0), [BF16], "rope", tab_a, ROT_A // 2)
    k_a32, k_a = _proj(xn, seg(1), [F32, BF16], "rope", tab_a, ROT_A // 2)
    v_a32, v_a = _proj(xn, seg(2), [F32, BF16])
    (q_i,) = _proj(xn, seg(3), [BF16], "rope", tab_i, ROT_IDX // 2)
    (q_b,) = _proj(xn, seg(6), [BF16])
    k_b32, k_b = _proj(xn, seg(7), [F32, BF16])
    v_b32, v_b = _proj(xn, seg(8), [F32, BF16])
    (gates,) = _proj(xn, w_in[:, offs[10]:offs[12]].astype(BF16), [F32], "sigmoid")

    n_small = D_IDX + H_IDX + h_b
    w_small = jnp.concatenate([w_in[:, offs[4]:offs[6]], w_in[:, offs[9]:offs[10]],
                               jnp.zeros((d, LANES - n_small), F32)], axis=1).astype(BF16)
    bias_row = jnp.zeros((1, LANES), F32).at[0, D_IDX + H_IDX:n_small].set(f_bias)
    small = _proj_small(xn, w_small, tab_k, bias_row)
    k_i = small[:, :D_IDX]
    w_i = small[:, D_IDX:D_IDX + H_IDX]
    log_f = small[:, D_IDX + H_IDX:n_small]
    return dict(q_a=q_a, k_a=k_a, v_a=v_a, k_a32=k_a32, v_a32=v_a32, q_i=q_i, k_i=k_i, w_i=w_i,
                q_b=q_b, k_b=k_b, v_b=v_b, k_b32=k_b32, v_b32=v_b32, log_f=log_f,
                g_a=gates[:, :d], g_b=gates[:, d:], gates=gates)


def _cumsum_kernel(x_ref, o_ref, *, blk):
    n = x_ref.shape[1]
    r = lax.broadcasted_iota(I32, (blk, blk), 0)
    c = lax.broadcasted_iota(I32, (blk, blk), 1)
    tri = (r <= c).astype(F32)

    def body(i, carry):
        off = pl.multiple_of(i * blk, blk)
        xb = x_ref[:, pl.ds(off, blk)]
        cs = jnp.dot(xb, tri, preferred_element_type=F32, precision=lax.Precision.HIGHEST) + carry
        o_ref[:, pl.ds(off, blk)] = cs
        return cs[:, blk - 1:blk]

    lax.fori_loop(0, n // blk, body, jnp.zeros((x_ref.shape[0], 1), F32))


def _cumsum_lanes(x_t, blk=LANES):
    return pl.pallas_call(
        functools.partial(_cumsum_kernel, blk=blk),
        out_shape=jax.ShapeDtypeStruct(x_t.shape, F32),
        compiler_params=_cparams(),
        name="cumsum_logf",
    )(x_t)


def _causal_steps(nq, tq, tk):
    qi_l, ki_l = [], []
    for qi in range(nq):
        for ki in range((qi * tq + tq - 1) // tk + 1):
            qi_l.append(qi)
            ki_l.append(ki)
    return jnp.asarray(np.array(qi_l, np.int32)), jnp.asarray(np.array(ki_l, np.int32))


def _softmax_step(s, h, sl, v_ref, m_ref, l_ref, acc_ref):
    m_prev = m_ref[h]
    m_new = jnp.maximum(m_prev, jnp.max(s, axis=1, keepdims=True))
    alpha = jnp.exp(m_prev - m_new)
    p = jnp.exp(s - m_new)
    l_ref[h] = alpha * l_ref[h] + jnp.sum(p, axis=1, keepdims=True)
    acc_ref[:, sl] = alpha * acc_ref[:, sl] + jnp.dot(p.astype(BF16), v_ref[:, sl], preferred_element_type=F32)
    m_ref[h] = m_new


def _fox_kernel(qi_ref, ki_ref, q_ref, k_ref, v_ref, ck_ref, o_ref, m_ref, l_ref, acc_ref, *, tq, tk, nh, scale):
    step = pl.program_id(0)
    qi, ki = qi_ref[step], ki_ref[step]

    @pl.when(ki == 0)
    def _():
        m_ref[...] = jnp.full(m_ref.shape, NEG, F32)
        l_ref[...] = jnp.zeros(l_ref.shape, F32)
        acc_ref[...] = jnp.zeros(acc_ref.shape, F32)

    qpos = qi * tq + lax.broadcasted_iota(I32, (tq, tk), 0)
    kpos = ki * tk + lax.broadcasted_iota(I32, (tq, tk), 1)
    causal = kpos <= qpos
    for h in range(nh):
        sl = slice(h * HEAD_DIM, (h + 1) * HEAD_DIM)
        s = lax.dot_general(q_ref[:, sl], k_ref[:, sl], NT_DIMS, preferred_element_type=F32)
        s = jnp.where(causal, s * scale - ck_ref[h:h + 1, :], NEG)
        _softmax_step(s, h, sl, v_ref, m_ref, l_ref, acc_ref)

    @pl.when(ki == (qi * tq + tq - 1) // tk)
    def _():
        for h in range(nh):
            sl = slice(h * HEAD_DIM, (h + 1) * HEAD_DIM)
            o_ref[:, sl] = (acc_ref[:, sl] / l_ref[h]).astype(o_ref.dtype)


def _fox_prompt(q, k, v, ck_t, tq=256, tk=512):
    t, w = q.shape
    nh = w // HEAD_DIM
    qi_arr, ki_arr = _causal_steps(t // tq, tq, tk)
    grid_spec = pltpu.PrefetchScalarGridSpec(
        num_scalar_prefetch=2,
        grid=(qi_arr.shape[0],),
        in_specs=[
            pl.BlockSpec((tq, w), lambda s, qi, ki: (qi[s], 0)),
            pl.BlockSpec((tk, w), lambda s, qi, ki: (ki[s], 0)),
            pl.BlockSpec((tk, w), lambda s, qi, ki: (ki[s], 0)),
            pl.BlockSpec((nh, tk), lambda s, qi, ki: (0, ki[s])),
        ],
        out_specs=pl.BlockSpec((tq, w), lambda s, qi, ki: (qi[s], 0)),
        scratch_shapes=[pltpu.VMEM((nh, tq, 1), F32), pltpu.VMEM((nh, tq, 1), F32), pltpu.VMEM((tq, w), F32)],
    )
    return pl.pallas_call(
        functools.partial(_fox_kernel, tq=tq, tk=tk, nh=nh, scale=HEAD_DIM ** -0.5),
        grid_spec=grid_spec,
        out_shape=jax.ShapeDtypeStruct((t, w), BF16),
        compiler_params=_cparams("arbitrary"),
        name="fox_prompt",
    )(qi_arr, ki_arr, q, k, v, ck_t)


def _sortable_key(score):
    bits = lax.bitcast_convert_type(score, I32)
    return jnp.where(bits < 0, bits ^ jnp.int32(0x7FFFFFFF), bits)


def _kth_largest_key(count_ge, shape, topk):
    au = jnp.zeros(shape, I32)
    for bit in range(31, -1, -1):
        bitval = jnp.int32(INT_MIN) if bit == 31 else jnp.int32(1 << bit)
        candu = au | bitval
        cnt = count_ge(candu ^ jnp.int32(INT_MIN))
        au = jnp.where(cnt >= topk, candu, au)
    return jnp.maximum(au ^ jnp.int32(INT_MIN), jnp.int32(INT_MIN + 1))


def _dsa_kernel(qi_ref, ki_ref, qa_ref, ka_ref, va_ref, qidx_ref, kidx_ref, wi_ref, o_ref,
                keys_ref, thr_ref, qs_ref, m_ref, l_ref, acc_ref, *, tq, tk, tc, nh, topk, scale):
    step = pl.program_id(0)
    qi, ki = qi_ref[step], ki_ref[step]
    last_ki = (qi * tq + tq - 1) // tk

    @pl.when(ki == 0)
    def _index_and_threshold():
        m_ref[...] = jnp.full(m_ref.shape, NEG, F32)
        l_ref[...] = jnp.zeros(l_ref.shape, F32)
        acc_ref[...] = jnp.zeros(acc_ref.shape, F32)
        for h in range(H_IDX):
            qs_ref[h * tq:(h + 1) * tq, :] = qidx_ref[:, h * D_IDX:(h + 1) * D_IDX]
        n_chunks = (last_ki + 1) * (tk // tc)
        qpos = qi * tq + lax.broadcasted_iota(I32, (tq, tc), 0)

        def score_chunk(c, carry):
            off = pl.multiple_of(c * tc, tc)
            s = lax.dot_general(qs_ref[...], kidx_ref[pl.ds(off, tc), :], NT_DIMS, preferred_element_type=F32)
            score = jnp.zeros((tq, tc), F32)
            for h in range(H_IDX):
                score = score + wi_ref[:, h:h + 1] * jnp.maximum(s[h * tq:(h + 1) * tq, :], 0.0)
            kpos = off + lax.broadcasted_iota(I32, (tq, tc), 1)
            keys_ref[:, pl.ds(off, tc)] = jnp.where(kpos <= qpos, _sortable_key(score), jnp.int32(INT_MIN))
            return carry

        lax.fori_loop(0, n_chunks, score_chunk, 0)

        def count_ge(cand):
            def body(c, acc):
                off = pl.multiple_of(c * tc, tc)
                ge = (keys_ref[:, pl.ds(off, tc)] >= cand).astype(I32)
                for b in range(tc // LANES):
                    acc = acc + ge[:, b * LANES:(b + 1) * LANES]
                return acc
            acc = lax.fori_loop(0, n_chunks, body, jnp.zeros((tq, LANES), I32))
            return jnp.sum(acc, axis=1, keepdims=True)

        thr_ref[...] = _kth_largest_key(count_ge, (tq, 1), topk)

    off_k = pl.multiple_of(ki * tk, tk)
    selected = keys_ref[:, pl.ds(off_k, tk)] >= thr_ref[...]
    for h in range(nh):
        sl = slice(h * HEAD_DIM, (h + 1) * HEAD_DIM)
        s = lax.dot_general(qa_ref[:, sl], ka_ref[:, sl], NT_DIMS, preferred_element_type=F32)
        s = jnp.where(selected, s * scale, NEG)
        _softmax_step(s, h, sl, va_ref, m_ref, l_ref, acc_ref)

    @pl.when(ki == last_ki)
    def _():
        for h in range(nh):
            sl = slice(h * HEAD_DIM, (h + 1) * HEAD_DIM)
            o_ref[:, sl] = (acc_ref[:, sl] / l_ref[h]).astype(o_ref.dtype)


def _dsa_prompt(q_a, k_a, v_a, q_i, k_i, w_i, tq=256, tk=512, tc=256):
    t, w = q_a.shape
    nh = w // HEAD_DIM
    topk = min(TOPK_MAX, t // 4)
    qi_arr, ki_arr = _causal_steps(t // tq, tq, tk)
    grid_spec = pltpu.PrefetchScalarGridSpec(
        num_scalar_prefetch=2,
        grid=(qi_arr.shape[0],),
        in_specs=[
            pl.BlockSpec((tq, w), lambda s, qi, ki: (qi[s], 0)),
            pl.BlockSpec((tk, w), lambda s, qi, ki: (ki[s], 0)),
            pl.BlockSpec((tk, w), lambda s, qi, ki: (ki[s], 0)),
            pl.BlockSpec((tq, H_IDX * D_IDX), lambda s, qi, ki: (qi[s], 0)),
            pl.BlockSpec((t, D_IDX), lambda s, qi, ki: (0, 0)),
            pl.BlockSpec((tq, H_IDX), lambda s, qi, ki: (qi[s], 0)),
        ],
        out_specs=pl.BlockSpec((tq, w), lambda s, qi, ki: (qi[s], 0)),
        scratch_shapes=[
            pltpu.VMEM((tq, t), I32),
            pltpu.VMEM((tq, 1), I32),
            pltpu.VMEM((H_IDX * tq, D_IDX), BF16),
            pltpu.VMEM((nh, tq, 1), F32), pltpu.VMEM((nh, tq, 1), F32), pltpu.VMEM((tq, w), F32),
        ],
    )
    return pl.pallas_call(
        functools.partial(_dsa_kernel, tq=tq, tk=tk, tc=tc, nh=nh, topk=topk, scale=HEAD_DIM ** -0.5),
        grid_spec=grid_spec,
        out_shape=jax.ShapeDtypeStruct((t, w), BF16),
        compiler_params=_cparams("arbitrary"),
        name="dsa_prompt",
    )(qi_arr, ki_arr, q_a, k_a, v_a, q_i, k_i.astype(BF16), w_i)


def _sample_index_kernel(pt_ref, q_ref, w_ref, knew_ref, *rest, n_pages, page, topk):
    kpages = rest[:n_pages]
    sel_ref, self_ref, key_scr = rest[n_pages], rest[n_pages + 1], rest[n_pages + 2]
    q = q_ref[...]
    w = w_ref[...]

    def score_of(kmat):
        s = lax.dot_general(q, kmat, NT_DIMS, preferred_element_type=F32)
        return jnp.sum(w * jnp.maximum(s, 0.0), axis=0, keepdims=True)

    for j in range(n_pages):
        key_scr[:, j * page:(j + 1) * page] = _sortable_key(score_of(kpages[j][...].astype(BF16)))
    knew = jnp.broadcast_to(knew_ref[...].astype(BF16), (8, D_IDX))
    self_key = _sortable_key(score_of(knew))[:, 0:1]

    def count_ge(cand):
        past = jnp.sum((key_scr[...] >= cand).astype(I32), axis=1, keepdims=True)
        return past + (self_key >= cand).astype(I32)

    thr = _kth_largest_key(count_ge, (1, 1), topk)
    sel_ref[...] = (key_scr[...] >= thr).astype(F32)
    self_ref[...] = jnp.broadcast_to((self_key >= thr).astype(F32), self_ref.shape)


def _sample_index(q_i, w_i, k_i_new, cache_kidx, page_table):
    n, n_pages = page_table.shape
    page = cache_kidx.shape[1]
    past = n_pages * page
    topk = min(TOPK_MAX, (past + 1) // 4)
    pt_flat = page_table.reshape(-1)
    kspecs = [pl.BlockSpec((None, page, D_IDX), functools.partial(lambda i, pt, j: (pt[i * n_pages + j], 0, 0), j=j))
              for j in range(n_pages)]
    grid_spec = pltpu.PrefetchScalarGridSpec(
        num_scalar_prefetch=1,
        grid=(n,),
        in_specs=[pl.BlockSpec((None, H_IDX, D_IDX), lambda i, pt: (i, 0, 0)),
                  pl.BlockSpec((None, H_IDX, 1), lambda i, pt: (i, 0, 0)),
                  pl.BlockSpec((None, 1, D_IDX), lambda i, pt: (i, 0, 0))] + kspecs,
        out_specs=[pl.BlockSpec((None, 1, past), lambda i, pt: (i, 0, 0)),
                   pl.BlockSpec((None, 1, LANES), lambda i, pt: (i, 0, 0))],
        scratch_shapes=[pltpu.VMEM((1, past), I32)],
    )
    sel, self_sel = pl.pallas_call(
        functools.partial(_sample_index_kernel, n_pages=n_pages, page=page, topk=topk),
        grid_spec=grid_spec,
        out_shape=[jax.ShapeDtypeStruct((n, 1, past), F32), jax.ShapeDtypeStruct((n, 1, LANES), F32)],
        compiler_params=_cparams("arbitrary"),
        name="sample_index",
    )(pt_flat, q_i.reshape(n, H_IDX, D_IDX), w_i.reshape(n, H_IDX, 1), k_i_new.reshape(n, 1, D_IDX),
      *([cache_kidx] * n_pages))
    return sel, self_sel


def _head_block_mask(nh, w):
    row = lax.broadcasted_iota(I32, (nh, w), 0)
    col = lax.broadcasted_iota(I32, (nh, w), 1)
    return (col // HEAD_DIM) == row


def _sample_attn_kernel(pt_ref, q_ref, knew_ref, vnew_ref, *rest, pps, page, nh, scale, fox):
    kp = rest[:pps]
    vp = rest[pps:2 * pps]
    if fox:
        lfp, lfnew_ref = rest[2 * pps:3 * pps], rest[3 * pps]
        o_ref, m_ref, l_ref, acc_ref, carry_ref = rest[3 * pps + 1:]
    else:
        sel_ref, self_ref = rest[2 * pps], rest[2 * pps + 1]
        o_ref, m_ref, l_ref, acc_ref = rest[2 * pps + 2:]
    p_step = pl.program_id(1)
    w = q_ref.shape[1]
    diag = _head_block_mask(nh, w)
    q_bd32 = jnp.where(diag, jnp.broadcast_to(q_ref[...].astype(F32), (nh, w)), 0.0)
    q_bd = q_bd32.astype(BF16)

    @pl.when(p_step == 0)
    def _():
        m_ref[...] = jnp.full(m_ref.shape, NEG, F32)
        l_ref[...] = jnp.zeros(l_ref.shape, F32)
        acc_ref[...] = jnp.zeros(acc_ref.shape, F32)
        if fox:
            carry_ref[...] = jnp.zeros(carry_ref.shape, F32)

    def update(s, vmat):
        m_prev = m_ref[...]
        m_new = jnp.maximum(m_prev, jnp.max(s, axis=1, keepdims=True))
        alpha = jnp.exp(m_prev - m_new)
        p = jnp.exp(s - m_new)
        l_ref[...] = alpha * l_ref[...] + jnp.sum(p, axis=1, keepdims=True)
        acc_ref[...] = alpha * acc_ref[...] + jnp.dot(p.astype(BF16), vmat, preferred_element_type=F32)
        m_ref[...] = m_new

    if fox:
        r = lax.broadcasted_iota(I32, (page, page), 0)
        c = lax.broadcasted_iota(I32, (page, page), 1)
        tri = (r <= c).astype(F32)
    for j in range(pps):
        s = lax.dot_general(q_bd, kp[j][...].astype(BF16), NT_DIMS, preferred_element_type=F32) * scale
        if fox:
            lf = lfp[j][...]
            prefix = jnp.dot(lf, tri, preferred_element_type=F32, precision=lax.Precision.HIGHEST) + carry_ref[...]
            carry_ref[...] = prefix[:, page - 1:page]
            s = s - prefix
        else:
            s = jnp.where(sel_ref[j] > 0.5, s, NEG)
        update(s, vp[j][...].astype(BF16))

    @pl.when(p_step == pl.num_programs(1) - 1)
    def _():
        knew = knew_ref[...].astype(BF16).astype(F32)
        s_self = jnp.sum(q_bd32 * knew, axis=1, keepdims=True) * scale
        if fox:
            s_self = s_self - (carry_ref[...] + lfnew_ref[...])
        else:
            s_self = jnp.where(self_ref[:, 0:1] > 0.5, s_self, NEG)
        m_prev = m_ref[...]
        m_new = jnp.maximum(m_prev, s_self)
        alpha = jnp.exp(m_prev - m_new)
        p = jnp.exp(s_self - m_new)
        l_fin = alpha * l_ref[...] + p
        vnew = vnew_ref[...].astype(BF16).astype(F32)
        acc = alpha * acc_ref[...] + p.astype(BF16).astype(F32) * vnew
        o = jnp.where(diag, acc / l_fin, 0.0)
        o_ref[...] = jnp.sum(o, axis=0, keepdims=True).astype(o_ref.dtype)


def _sample_attn(q, k_new, v_new, cache_k, cache_v, page_table, extra, fox, pps=4):
    n, w = q.shape
    nh = w // HEAD_DIM
    n_pages = page_table.shape[1]
    page = cache_k.shape[1]
    pt_flat = page_table.reshape(-1)

    def page_spec(j):
        return pl.BlockSpec((None, page, w), lambda i, p, pt: (pt[i * n_pages + p * pps + j], 0, 0))

    row3 = lambda a: a.reshape(n, 1, a.shape[-1])
    in_specs = [pl.BlockSpec((None, 1, w), lambda i, p, pt: (i, 0, 0))] * 3
    in_specs += [page_spec(j) for j in range(pps)] * 2
    args = [row3(q), row3(k_new), row3(v_new)] + [cache_k] * pps + [cache_v] * pps
    scratch = [pltpu.VMEM((nh, 1), F32), pltpu.VMEM((nh, 1), F32), pltpu.VMEM((nh, w), F32)]
    if fox:
        cache_lf_t, lf_new = extra

        def lf_spec(j):
            return pl.BlockSpec((None, nh, page), lambda i, p, pt: (pt[i * n_pages + p * pps + j], 0, 0))

        in_specs += [lf_spec(j) for j in range(pps)] + [pl.BlockSpec((None, nh, 1), lambda i, p, pt: (i, 0, 0))]
        args += [cache_lf_t] * pps + [lf_new]
        scratch += [pltpu.VMEM((nh, 1), F32)]
    else:
        sel, self_sel = extra
        in_specs += [pl.BlockSpec((None, pps, 1, page), lambda i, p, pt: (i, p, 0, 0)),
                     pl.BlockSpec((None, 1, LANES), lambda i, p, pt: (i, 0, 0))]
        args += [sel, self_sel]
    grid_spec = pltpu.PrefetchScalarGridSpec(
        num_scalar_prefetch=1,
        grid=(n, n_pages // pps),
        in_specs=in_specs,
        out_specs=pl.BlockSpec((None, 1, w), lambda i, p, pt: (i, 0, 0)),
        scratch_shapes=scratch,
    )
    out = pl.pallas_call(
        functools.partial(_sample_attn_kernel, pps=pps, page=page, nh=nh, scale=HEAD_DIM ** -0.5, fox=fox),
        grid_spec=grid_spec,
        out_shape=jax.ShapeDtypeStruct((n, 1, w), BF16),
        compiler_params=_cparams("arbitrary", "arbitrary"),
        name="sample_fox" if fox else "sample_dsa",
    )(pt_flat, *args)
    return out.reshape(n, w)


def _merge_kernel(x_ref, oa_ref, ob_ref, g_ref, wao_ref, wbo_ref, wout_ref, gffn_ref, wr_ref,
                  h_ref, hn_ref, eid_ref, gate_ref):
    d = x_ref.shape[1]
    br_a = jnp.dot(oa_ref[...], wao_ref[...], preferred_element_type=F32)
    br_b = jnp.dot(ob_ref[...], wbo_ref[...], preferred_element_type=F32)
    mix = g_ref[:, :d] * br_a + g_ref[:, d:] * br_b
    h = x_ref[...] + jnp.dot(mix.astype(BF16), wout_ref[...], preferred_element_type=F32)
    h_ref[...] = h
    hn = h * lax.rsqrt(jnp.mean(h * h, axis=-1, keepdims=True) + EPS) * gffn_ref[...]
    hn_ref[...] = hn

    logits = lax.dot_general(wr_ref[...], hn, NT_DIMS, preferred_element_type=F32,
                             precision=lax.Precision.HIGHEST)
    tm = logits.shape[1]
    gl = logits[0:N_GROUPS]
    row_g = lax.broadcasted_iota(I32, gl.shape, 0)
    gmax = jnp.max(gl, axis=0, keepdims=True)
    g_idx = jnp.min(jnp.where(gl == gmax, row_g, N_GROUPS), axis=0, keepdims=True)
    g_w = 1.0 / jnp.sum(jnp.exp(gl - gmax), axis=0, keepdims=True)
    in_l = jnp.zeros((EXPERTS_PER_GROUP, tm), F32)
    for g in range(N_GROUPS):
        lo = 8 + g * EXPERTS_PER_GROUP
        in_l = in_l + jnp.where(g_idx == g, logits[lo:lo + EXPERTS_PER_GROUP], 0.0)
    e = jnp.exp(in_l - jnp.max(in_l, axis=0, keepdims=True))
    in_p = e / jnp.sum(e, axis=0, keepdims=True)
    row_e = lax.broadcasted_iota(I32, in_p.shape, 0)
    p1 = jnp.max(in_p, axis=0, keepdims=True)
    i1 = jnp.min(jnp.where(in_p == p1, row_e, EXPERTS_PER_GROUP), axis=0, keepdims=True)
    rest = jnp.where(row_e == i1, -1.0, in_p)
    p2 = jnp.max(rest, axis=0, keepdims=True)
    i2 = jnp.min(jnp.where(rest == p2, row_e, EXPERTS_PER_GROUP), axis=0, keepdims=True)
    denom = p1 + p2
    row8 = lax.broadcasted_iota(I32, (8, tm), 0)
    eid_ref[...] = jnp.where(row8 == 0, g_idx * EXPERTS_PER_GROUP + i1,
                             jnp.where(row8 == 1, g_idx * EXPERTS_PER_GROUP + i2, 0))
    gate_ref[...] = jnp.where(row8 == 0, p1 / denom * g_w, jnp.where(row8 == 1, p2 / denom * g_w, 0.0))


def _merge(x2d, o_a, o_b, gates, w_a_o, w_b_o, w_out, g_ffn, wr_t, tm=256):
    m, d = x2d.shape
    tm = min(tm, m)
    wa = o_a.shape[1]
    const = lambda shape: pl.BlockSpec(shape, lambda i: (0, 0), pipeline_mode=pl.Buffered(1))
    return pl.pallas_call(
        _merge_kernel,
        grid=(m // tm,),
        in_specs=[pl.BlockSpec((tm, d), lambda i: (i, 0)),
                  pl.BlockSpec((tm, wa), lambda i: (i, 0)),
                  pl.BlockSpec((tm, wa), lambda i: (i, 0)),
                  pl.BlockSpec((tm, 2 * d), lambda i: (i, 0)),
                  const((wa, d)), const((wa, d)), const((d, d)), const((1, d)), const(wr_t.shape)],
        out_specs=[pl.BlockSpec((tm, d), lambda i: (i, 0)), pl.BlockSpec((tm, d), lambda i: (i, 0)),
                   pl.BlockSpec((8, tm), lambda i: (0, i)), pl.BlockSpec((8, tm), lambda i: (0, i))],
        out_shape=[jax.ShapeDtypeStruct((m, d), F32), jax.ShapeDtypeStruct((m, d), F32),
                   jax.ShapeDtypeStruct((8, m), I32), jax.ShapeDtypeStruct((8, m), F32)],
        compiler_params=_cparams("parallel"),
        name="merge_router",
    )(x2d, o_a, o_b, gates, w_a_o, w_b_o, w_out, g_ffn.reshape(1, d), wr_t)


def _dispatch_kernel(dest_ref, hp_ref, hs_ref, xs_in_ref, xs_ref, sem, *, m_p, m_s, max_chunk):
    del xs_in_ref
    m_tot = m_p + m_s

    def copies(src_ref, row, tok):
        return [pltpu.make_async_copy(src_ref.at[pl.ds(row, 1)], xs_ref.at[pl.ds(dest_ref[k * m_tot + tok], 1)], sem)
                for k in range(2)]

    def run(src_ref, n_rows, tok_off):
        chunk = min(max_chunk, n_rows)
        assert n_rows % chunk == 0
        n_chunks = n_rows // chunk

        def start_chunk(c):
            def body(r, carry):
                for cp in copies(src_ref, c * chunk + r, tok_off + c * chunk + r):
                    cp.start()
                return carry
            lax.fori_loop(0, chunk, body, 0)

        def wait_chunk(c):
            def body(r, carry):
                for cp in copies(src_ref, c * chunk + r, tok_off + c * chunk + r):
                    cp.wait()
                return carry
            lax.fori_loop(0, chunk, body, 0)

        start_chunk(0)

        def step(c, carry):
            @pl.when(c + 1 < n_chunks)
            def _():
                start_chunk(c + 1)
            wait_chunk(c)
            return carry

        lax.fori_loop(0, n_chunks, step, 0)

    run(hp_ref, m_p, 0)
    run(hs_ref, m_s, m_p)


def _dispatch(dest, hn_p, hn_s, n_rows):
    m_p, d = hn_p.shape
    m_s = hn_s.shape[0]
    grid_spec = pltpu.PrefetchScalarGridSpec(
        num_scalar_prefetch=1,
        grid=(1,),
        in_specs=[pl.BlockSpec(memory_space=pl.ANY)] * 3,
        out_specs=pl.BlockSpec(memory_space=pl.ANY),
        scratch_shapes=[pltpu.SemaphoreType.DMA(())],
    )
    return pl.pallas_call(
        functools.partial(_dispatch_kernel, m_p=m_p, m_s=m_s, max_chunk=32),
        grid_spec=grid_spec,
        out_shape=jax.ShapeDtypeStruct((n_rows, d), F32),
        input_output_aliases={3: 0},
        compiler_params=_cparams("arbitrary"),
        name="moe_dispatch",
    )(dest, hn_p, hn_s, jnp.zeros((n_rows, d), F32))


def _expert_kernel(te_ref, nv_ref, x_ref, w1_ref, w3_ref, wd_ref, y_ref, w1_s, w3_s, wd_s):
    i = pl.program_id(0)
    changed = jnp.logical_or(i == 0, te_ref[i] != te_ref[jnp.maximum(i - 1, 0)])

    @pl.when(changed)
    def _():
        w1_s[...] = w1_ref[...].astype(BF16)
        w3_s[...] = w3_ref[...].astype(BF16)
        wd_s[...] = wd_ref[...].astype(BF16)

    @pl.when(i < nv_ref[0])
    def _():
        x = x_ref[...].astype(BF16)
        u1 = jnp.dot(x, w1_s[...], preferred_element_type=F32)
        u3 = jnp.dot(x, w3_s[...], preferred_element_type=F32)
        hidden = u1 / (1.0 + jnp.exp(-u1)) * u3
        y_ref[...] = jnp.dot(hidden.astype(BF16), wd_s[...], preferred_element_type=F32)

    @pl.when(i >= nv_ref[0])
    def _():
        y_ref[...] = jnp.zeros(y_ref.shape, F32)


def _experts(tile_expert, n_valid, xs, w_up1, w_up3, w_down, tm):
    n_rows, d = xs.shape
    f = w_up1.shape[2]
    n_tiles = n_rows // tm
    row_map = lambda i, te, nv: (jnp.minimum(i, nv[0] - 1), 0)
    grid_spec = pltpu.PrefetchScalarGridSpec(
        num_scalar_prefetch=2,
        grid=(n_tiles,),
        in_specs=[pl.BlockSpec((tm, d), row_map),
                  pl.BlockSpec((None, d, f), lambda i, te, nv: (te[i], 0, 0)),
                  pl.BlockSpec((None, d, f), lambda i, te, nv: (te[i], 0, 0)),
                  pl.BlockSpec((None, f, d), lambda i, te, nv: (te[i], 0, 0))],
        out_specs=pl.BlockSpec((tm, d), lambda i, te, nv: (i, 0)),
        scratch_shapes=[pltpu.VMEM((d, f), BF16), pltpu.VMEM((d, f), BF16), pltpu.VMEM((f, d), BF16)],
    )
    return pl.pallas_call(
        _expert_kernel,
        grid_spec=grid_spec,
        out_shape=jax.ShapeDtypeStruct((n_rows, d), F32),
        compiler_params=_cparams("arbitrary"),
        name="moe_experts",
    )(tile_expert, n_valid, xs, w_up1, w_up3, w_down)


def _combine_kernel(dest_ref, h_ref, gate_ref, gfin_ref, ys_ref, y_ref, buf0, buf1, sem, *, tok_off, m_tot, tm):
    base = tok_off + pl.program_id(0) * tm

    def copies(r):
        return (pltpu.make_async_copy(ys_ref.at[pl.ds(dest_ref[base + r], 1)], buf0.at[pl.ds(r, 1)], sem),
                pltpu.make_async_copy(ys_ref.at[pl.ds(dest_ref[m_tot + base + r], 1)], buf1.at[pl.ds(r, 1)], sem))

    def start(r, c):
        for cp in copies(r):
            cp.start()
        return c

    def wait(r, c):
        for cp in copies(r):
            cp.wait()
        return c

    lax.fori_loop(0, tm, start, 0)
    lax.fori_loop(0, tm, wait, 0)
    out = h_ref[...] + gate_ref[:, 0:1] * buf0[...] + gate_ref[:, 1:2] * buf1[...]
    y = out * lax.rsqrt(jnp.mean(out * out, axis=-1, keepdims=True) + EPS)
    y_ref[...] = y * gfin_ref[...]


def _combine(dest, h, gate_cols, g_final, ys, tok_off, m_tot, tm=128):
    m, d = h.shape
    tm = min(tm, m)
    grid_spec = pltpu.PrefetchScalarGridSpec(
        num_scalar_prefetch=1,
        grid=(m // tm,),
        in_specs=[pl.BlockSpec((tm, d), lambda i, dr: (i, 0)),
                  pl.BlockSpec((tm, 2), lambda i, dr: (i, 0)),
                  pl.BlockSpec((1, d), lambda i, dr: (0, 0)),
                  pl.BlockSpec(memory_space=pl.ANY)],
        out_specs=pl.BlockSpec((tm, d), lambda i, dr: (i, 0)),
        scratch_shapes=[pltpu.VMEM((tm, d), F32), pltpu.VMEM((tm, d), F32), pltpu.SemaphoreType.DMA(())],
    )
    return pl.pallas_call(
        functools.partial(_combine_kernel, tok_off=tok_off, m_tot=m_tot, tm=tm),
        grid_spec=grid_spec,
        out_shape=jax.ShapeDtypeStruct((m, d), F32),
        compiler_params=_cparams("arbitrary"),
        name="moe_combine",
    )(dest, h, gate_cols, g_final.reshape(1, d), ys)


def _moe_plan(eids, tm):
    e_flat = eids.reshape(-1)
    n_assign = e_flat.shape[0]
    onehot = (e_flat[:, None] == jnp.arange(N_EXPERTS, dtype=I32)[None, :]).astype(I32)
    csum = jnp.cumsum(onehot, axis=0)
    rank = jnp.sum((csum - onehot) * onehot, axis=1)
    counts = csum[-1]
    tiles_e = (counts + tm - 1) // tm
    tile_end = jnp.cumsum(tiles_e)
    dest = ((tile_end - tiles_e) * tm)[e_flat] + rank
    n_tiles = n_assign // tm + N_EXPERTS
    n_valid = tile_end[-1]
    tile_ids = jnp.minimum(jnp.arange(n_tiles, dtype=I32), n_valid - 1)
    tile_expert = jnp.sum((tile_end[None, :] <= tile_ids[:, None]).astype(I32), axis=1)
    return dest.astype(I32), tile_expert.astype(I32), n_valid.reshape(1).astype(I32), n_tiles


def kernel(x_prompt, x_sample, cache_k_a, cache_v_a, cache_kidx_a, cache_k_b, cache_v_b, cache_logf_b,
           page_table, g_mix, w_in, f_bias, w_a_o, w_b_o, w_out, g_ffn, w_grp, w_exp, w_up1, w_up3,
           w_down, g_final):
    depth = w_in.shape[0]
    assert depth == 1, "single-layer stack"
    batch, seq, d = x_prompt.shape
    n_dec, dec_seq, _ = x_sample.shape
    assert batch == 1 and dec_seq == 1
    n_pool, page = cache_k_a.shape[1], cache_k_a.shape[2]
    n_pages = page_table.shape[1]
    past = n_pages * page
    l = 0

    pp = _project_group(x_prompt.reshape(seq, d), jnp.arange(seq), g_mix[l], w_in[l], f_bias[l])
    ps = _project_group(x_sample.reshape(n_dec, d), jnp.full((n_dec,), past, I32), g_mix[l], w_in[l], f_bias[l])
    w_a = pp["q_a"].shape[1]
    h_a = h_b = w_a // HEAD_DIM

    o_a_p = _dsa_prompt(pp["q_a"], pp["k_a"], pp["v_a"], pp["q_i"], pp["k_i"], pp["w_i"])
    ck_t = _cumsum_lanes(pp["log_f"].T)
    o_b_p = _fox_prompt(pp["q_b"], pp["k_b"], pp["v_b"], ck_t)

    sel, self_sel = _sample_index(ps["q_i"], ps["w_i"], ps["k_i"], cache_kidx_a[l], page_table)
    o_a_s = _sample_attn(ps["q_a"], ps["k_a32"], ps["v_a32"], cache_k_a[l].reshape(n_pool, page, w_a),
                         cache_v_a[l].reshape(n_pool, page, w_a), page_table,
                         (sel.reshape(n_dec, n_pages, 1, page), self_sel), fox=False)
    cache_lf_t = jnp.swapaxes(cache_logf_b[l], 1, 2)
    o_b_s = _sample_attn(ps["q_b"], ps["k_b32"], ps["v_b32"], cache_k_b[l].reshape(n_pool, page, w_a),
                         cache_v_b[l].reshape(n_pool, page, w_a), page_table,
                         (cache_lf_t, ps["log_f"].reshape(n_dec, h_b, 1)), fox=True)

    wr_t = jnp.zeros((8 + N_EXPERTS, d), F32).at[:N_GROUPS].set(w_grp[l].T).at[8:].set(w_exp[l].T)
    wao, wbo, wo = w_a_o[l].astype(BF16), w_b_o[l].astype(BF16), w_out[l].astype(BF16)
    h_p, hn_p, eid_p, gate_p = _merge(x_prompt.reshape(seq, d), o_a_p, o_b_p, pp["gates"], wao, wbo, wo, g_ffn[l], wr_t)
    h_s, hn_s, eid_s, gate_s = _merge(x_sample.reshape(n_dec, d), o_a_s, o_b_s, ps["gates"], wao, wbo, wo, g_ffn[l], wr_t)

    tm_e = 256
    m_tot = seq + n_dec
    eids = jnp.concatenate([eid_p[:2], eid_s[:2]], axis=1)
    dest, tile_expert, n_valid, n_tiles = _moe_plan(eids, tm_e)
    xs = _dispatch(dest, hn_p, hn_s, n_tiles * tm_e)
    ys = _experts(tile_expert, n_valid, xs, w_up1[l], w_up3[l], w_down[l], tm_e)
    y_p = _combine(dest, h_p, gate_p[:2].T, g_final, ys, 0, m_tot)
    y_s = _combine(dest, h_s, gate_s[:2].T, g_final, ys, seq, m_tot)

    kv = lambda a, b, t, nh: a.reshape(depth, b, t, nh, HEAD_DIM)
    return (y_p.reshape(batch, seq, d), y_s.reshape(n_dec, dec_seq, d),
            kv(pp["k_a32"], batch, seq, h_a), kv(pp["v_a32"], batch, seq, h_a),
            pp["k_i"].reshape(depth, batch, seq, D_IDX),
            kv(pp["k_b32"], batch, seq, h_b), kv(pp["v_b32"], batch, seq, h_b),
            pp["log_f"].reshape(depth, batch, seq, h_b),
            kv(ps["k_a32"], n_dec, dec_seq, h_a), kv(ps["v_a32"], n_dec, dec_seq, h_a),
            ps["k_i"].reshape(depth, n_dec, dec_seq, D_IDX),
            kv(ps["k_b32"], n_dec, dec_seq, h_b), kv(ps["v_b32"], n_dec, dec_seq, h_b),
            ps["log_f"].reshape(depth, n_dec, dec_seq, h_b))
```

```python
import functools

import numpy as np
import jax
import jax.numpy as jnp
from jax import lax
from jax.experimental import pallas as pl
from jax.experimental.pallas import tpu as pltpu

F32 = jnp.float32
BF16 = jnp.bfloat16
I32 = jnp.int32

HEAD_DIM = 128
H_IDX = 16
D_IDX = 64
TOPK_MAX = 256
ROT_A = HEAD_DIM // 4
ROT_IDX = D_IDX // 4
ROPE_THETA = 500000.0
N_GROUPS = 4
EXPERTS_PER_GROUP = 8
N_EXPERTS = N_GROUPS * EXPERTS_PER_GROUP
EPS = 1e-6

LANES = 128
NEG = -1e30
INT_MIN = -(2 ** 31)
VMEM_LIMIT = 56 * 1024 * 1024

NT_DIMS = (((1,), (1,)), ((), ()))


def _cparams(*sem, vmem=VMEM_LIMIT):
    return pltpu.CompilerParams(dimension_semantics=sem, vmem_limit_bytes=vmem)


def _rmsnorm_kernel(x_ref, g_ref, o_ref):
    x = x_ref[...]
    y = x * lax.rsqrt(jnp.mean(x * x, axis=-1, keepdims=True) + EPS)
    o_ref[...] = (y * g_ref[...]).astype(o_ref.dtype)


def _rmsnorm(x, g, out_dtype, tm):
    m, d = x.shape
    return pl.pallas_call(
        _rmsnorm_kernel,
        grid=(m // tm,),
        in_specs=[pl.BlockSpec((tm, d), lambda i: (i, 0)), pl.BlockSpec((1, d), lambda i: (0, 0))],
        out_specs=pl.BlockSpec((tm, d), lambda i: (i, 0)),
        out_shape=jax.ShapeDtypeStruct((m, d), out_dtype),
        compiler_params=_cparams("parallel"),
        name="rmsnorm",
    )(x, g.reshape(1, d))


def _rope_tables(pos, rot_dim, period, active_lanes):
    half = rot_dim // 2
    inv_freq = ROPE_THETA ** (-jnp.arange(half, dtype=F32) / half)
    ang = pos.astype(F32)[:, None] * inv_freq[None, :]
    cos, sin = jnp.cos(ang), jnp.sin(ang)
    lane = np.arange(LANES)
    d = lane % period
    first = (d < half) & (lane < active_lanes)
    second = (d >= half) & (d < rot_dim) & (lane < active_lanes)
    idx = np.where(first, d, np.where(second, d - half, 0))
    cos_l, sin_l = cos[:, idx], sin[:, idx]
    rot = jnp.asarray(first | second)[None, :]
    c = jnp.where(rot, cos_l, 1.0)
    s1 = jnp.where(jnp.asarray(first)[None, :], -sin_l, 0.0)
    s2 = jnp.where(jnp.asarray(second)[None, :], sin_l, 0.0)
    return c, s1, s2


def _rope_lanes(y, c, s1, s2, half):
    return y * c + pltpu.roll(y, LANES - half, 1) * s1 + pltpu.roll(y, half, 1) * s2


def _proj_kernel(*refs, mode, half, n_out):
    x_ref, w_ref = refs[0], refs[1]
    outs = refs[len(refs) - n_out:]
    acc = jnp.dot(x_ref[...], w_ref[...], preferred_element_type=F32)
    tn = acc.shape[1]
    if mode == "rope":
        c, s1, s2 = refs[2][...], refs[3][...], refs[4][...]
        for b in range(tn // LANES):
            sl = slice(b * LANES, (b + 1) * LANES)
            y = _rope_lanes(acc[:, sl], c, s1, s2, half)
            for o in outs:
                o[:, sl] = y.astype(o.dtype)
        return
    if mode == "sigmoid":
        acc = 1.0 / (1.0 + jnp.exp(-acc))
    for o in outs:
        o[...] = acc.astype(o.dtype)


def _proj(xn, w, out_dtypes, mode="none", tables=None, half=0, tm=512, tn=1024):
    m, d = xn.shape
    n = w.shape[1]
    tm = min(tm, m)
    in_specs = [pl.BlockSpec((tm, d), lambda j, i: (i, 0)), pl.BlockSpec((d, tn), lambda j, i: (0, j))]
    args = [xn, w]
    if mode == "rope":
        in_specs += [pl.BlockSpec((tm, LANES), lambda j, i: (i, 0))] * 3
        args += list(tables)
    outs = pl.pallas_call(
        functools.partial(_proj_kernel, mode=mode, half=half, n_out=len(out_dtypes)),
        grid=(n // tn, m // tm),
        in_specs=in_specs,
        out_specs=[pl.BlockSpec((tm, tn), lambda j, i: (i, j)) for _ in out_dtypes],
        out_shape=[jax.ShapeDtypeStruct((m, n), dt) for dt in out_dtypes],
        compiler_params=_cparams("parallel", "parallel"),
        name="proj_" + mode,
    )(*args)
    return outs


def _proj_small_kernel(x_ref, w_ref, c_ref, s1_ref, s2_ref, bias_ref, o_ref, *, half, wi_scale):
    acc = jnp.dot(x_ref[...], w_ref[...], preferred_element_type=F32)
    roped = _rope_lanes(acc, c_ref[...], s1_ref[...], s2_ref[...], half)
    z = acc + bias_ref[...]
    logsig = -(jnp.maximum(-z, 0.0) + jnp.log(1.0 + jnp.exp(-jnp.abs(z))))
    lane = lax.broadcasted_iota(I32, acc.shape, 1)
    o_ref[...] = jnp.where(lane < D_IDX, roped, jnp.where(lane < D_IDX + H_IDX, acc * wi_scale, logsig))


def _proj_small(xn, w_small, tables, bias_row, tm=512):
    m, d = xn.shape
    tm = min(tm, m)
    wi_scale = (H_IDX ** -0.5) * (D_IDX ** -0.5)
    return pl.pallas_call(
        functools.partial(_proj_small_kernel, half=ROT_IDX // 2, wi_scale=wi_scale),
        grid=(m // tm,),
        in_specs=[pl.BlockSpec((tm, d), lambda i: (i, 0)), pl.BlockSpec((d, LANES), lambda i: (0, 0))]
        + [pl.BlockSpec((tm, LANES), lambda i: (i, 0))] * 3 + [pl.BlockSpec((1, LANES), lambda i: (0, 0))],
        out_specs=pl.BlockSpec((tm, LANES), lambda i: (i, 0)),
        out_shape=jax.ShapeDtypeStruct((m, LANES), F32),
        compiler_params=_cparams("parallel"),
        name="proj_small",
    )(xn, w_small, *tables, bias_row)


def _project_group(x2d, pos, g_mix, w_in, f_bias):
    m, d = x2d.shape
    h_a = h_b = (w_in.shape[1] - (H_IDX * D_IDX + D_IDX + H_IDX) - 2 * d) // (6 * HEAD_DIM + 1)
    w_a = h_a * HEAD_DIM
    sizes = (w_a, w_a, w_a, H_IDX * D_IDX, D_IDX, H_IDX, w_a, w_a, w_a, h_b, d, d)
    offs = np.concatenate([[0], np.cumsum(sizes)])
    seg = lambda k: w_in[:, offs[k]:offs[k + 1]].astype(BF16)

    xn = _rmsnorm(x2d, g_mix, BF16, tm=min(512, m))
    tab_a = _rope_tables(pos, ROT_A, HEAD_DIM, LANES)
    tab_i = _rope_tables(pos, ROT_IDX, D_IDX, LANES)
    tab_k = _rope_tables(pos, ROT_IDX, D_IDX, ROT_IDX)

    (q_a,) = _proj(xn, seg(0), [BF16], "rope", tab_a, ROT_A // 2)
    k_a32, k_a = _proj(xn, seg(1), [F32, BF16], "rope", tab_a, ROT_A // 2)
    v_a32, v_a = _proj(xn, seg(2), [F32, BF16])
    (q_i,) = _proj(xn, seg(3), [BF16], "rope", tab_i, ROT_IDX // 2)
    (q_b,) = _proj(xn, seg(6), [BF16])
    k_b32, k_b = _proj(xn, seg(7), [F32, BF16])
    v_b32, v_b = _proj(xn, seg(8), [F32, BF16])
    (gates,) = _proj(xn, w_in[:, offs[10]:offs[12]].astype(BF16), [F32], "sigmoid")

    n_small = D_IDX + H_IDX + h_b
    w_small = jnp.concatenate([w_in[:, offs[4]:offs[6]], w_in[:, offs[9]:offs[10]],
                               jnp.zeros((d, LANES - n_small), F32)], axis=1).astype(BF16)
    bias_row = jnp.zeros((1, LANES), F32).at[0, D_IDX + H_IDX:n_small].set(f_bias)
    small = _proj_small(xn, w_small, tab_k, bias_row)
    k_i = small[:, :D_IDX]
    w_i = small[:, D_IDX:D_IDX + H_IDX]
    log_f = small[:, D_IDX + H_IDX:n_small]
    return dict(q_a=q_a, k_a=k_a, v_a=v_a, k_a32=k_a32, v_a32=v_a32, q_i=q_i, k_i=k_i, w_i=w_i,
                q_b=q_b, k_b=k_b, v_b=v_b, k_b32=k_b32, v_b32=v_b32, log_f=log_f,
                g_a=gates[:, :d], g_b=gates[:, d:], gates=gates)


def _cumsum_kernel(x_ref, o_ref, *, blk):
    n = x_ref.shape[1]
    r = lax.broadcasted_iota(I32, (blk, blk), 0)
    c = lax.broadcasted_iota(I32, (blk, blk), 1)
    tri = (r <= c).astype(F32)

    def body(i, carry):
        off = pl.multiple_of(i * blk, blk)
        xb = x_ref[:, pl.ds(off, blk)]
        cs = jnp.dot(xb, tri, preferred_element_type=F32, precision=lax.Precision.HIGHEST) + carry
        o_ref[:, pl.ds(off, blk)] = cs
        return cs[:, blk - 1:blk]

    lax.fori_loop(0, n // blk, body, jnp.zeros((x_ref.shape[0], 1), F32))


def _cumsum_lanes(x_t, blk=LANES):
    return pl.pallas_call(
        functools.partial(_cumsum_kernel, blk=blk),
        out_shape=jax.ShapeDtypeStruct(x_t.shape, F32),
        compiler_params=_cparams(),
        name="cumsum_logf",
    )(x_t)


def _causal_steps(nq, tq, tk):
    qi_l, ki_l = [], []
    for qi in range(nq):
        for ki in range((qi * tq + tq - 1) // tk + 1):
            qi_l.append(qi)
            ki_l.append(ki)
    return jnp.asarray(np.array(qi_l, np.int32)), jnp.asarray(np.array(ki_l, np.int32))


def _softmax_step(s, h, sl, v_ref, m_ref, l_ref, acc_ref):
    m_prev = m_ref[h]
    m_new = jnp.maximum(m_prev, jnp.max(s, axis=1, keepdims=True))
    alpha = jnp.exp(m_prev - m_new)
    p = jnp.exp(s - m_new)
    l_ref[h] = alpha * l_ref[h] + jnp.sum(p, axis=1, keepdims=True)
    acc_ref[:, sl] = alpha * acc_ref[:, sl] + jnp.dot(p.astype(BF16), v_ref[:, sl], preferred_element_type=F32)
    m_ref[h] = m_new


def _fox_kernel(qi_ref, ki_ref, q_ref, k_ref, v_ref, ck_ref, o_ref, m_ref, l_ref, acc_ref, *, tq, tk, nh, scale):
    step = pl.program_id(0)
    qi, ki = qi_ref[step], ki_ref[step]

    @pl.when(ki == 0)
    def _():
        m_ref[...] = jnp.full(m_ref.shape, NEG, F32)
        l_ref[...] = jnp.zeros(l_ref.shape, F32)
        acc_ref[...] = jnp.zeros(acc_ref.shape, F32)

    qpos = qi * tq + lax.broadcasted_iota(I32, (tq, tk), 0)
    kpos = ki * tk + lax.broadcasted_iota(I32, (tq, tk), 1)
    causal = kpos <= qpos
    for h in range(nh):
        sl = slice(h * HEAD_DIM, (h + 1) * HEAD_DIM)
        s = lax.dot_general(q_ref[:, sl], k_ref[:, sl], NT_DIMS, preferred_element_type=F32)
        s = jnp.where(causal, s * scale - ck_ref[h:h + 1, :], NEG)
        _softmax_step(s, h, sl, v_ref, m_ref, l_ref, acc_ref)

    @pl.when(ki == (qi * tq + tq - 1) // tk)
    def _():
        for h in range(nh):
            sl = slice(h * HEAD_DIM, (h + 1) * HEAD_DIM)
            o_ref[:, sl] = (acc_ref[:, sl] / l_ref[h]).astype(o_ref.dtype)


def _fox_prompt(q, k, v, ck_t, tq=256, tk=512):
    t, w = q.shape
    nh = w // HEAD_DIM
    qi_arr, ki_arr = _causal_steps(t // tq, tq, tk)
    grid_spec = pltpu.PrefetchScalarGridSpec(
        num_scalar_prefetch=2,
        grid=(qi_arr.shape[0],),
        in_specs=[
            pl.BlockSpec((tq, w), lambda s, qi, ki: (qi[s], 0)),
            pl.BlockSpec((tk, w), lambda s, qi, ki: (ki[s], 0)),
            pl.BlockSpec((tk, w), lambda s, qi, ki: (ki[s], 0)),
            pl.BlockSpec((nh, tk), lambda s, qi, ki: (0, ki[s])),
        ],
        out_specs=pl.BlockSpec((tq, w), lambda s, qi, ki: (qi[s], 0)),
        scratch_shapes=[pltpu.VMEM((nh, tq, 1), F32), pltpu.VMEM((nh, tq, 1), F32), pltpu.VMEM((tq, w), F32)],
    )
    return pl.pallas_call(
        functools.partial(_fox_kernel, tq=tq, tk=tk, nh=nh, scale=HEAD_DIM ** -0.5),
        grid_spec=grid_spec,
        out_shape=jax.ShapeDtypeStruct((t, w), BF16),
        compiler_params=_cparams("arbitrary"),
        name="fox_prompt",
    )(qi_arr, ki_arr, q, k, v, ck_t)


def _sortable_key(score):
    bits = lax.bitcast_convert_type(score, I32)
    return jnp.where(bits < 0, bits ^ jnp.int32(0x7FFFFFFF), bits)


def _kth_largest_key(count_ge, shape, topk):
    au = jnp.zeros(shape, I32)
    for bit in range(31, -1, -1):
        bitval = jnp.int32(INT_MIN) if bit == 31 else jnp.int32(1 << bit)
        candu = au | bitval
        cnt = count_ge(candu ^ jnp.int32(INT_MIN))
        au = jnp.where(cnt >= topk, candu, au)
    return jnp.maximum(au ^ jnp.int32(INT_MIN), jnp.int32(INT_MIN + 1))


def _dsa_kernel(qi_ref, ki_ref, qa_ref, ka_ref, va_ref, qidx_ref, kidx_ref, wi_ref, o_ref,
                keys_ref, thr_ref, qs_ref, m_ref, l_ref, acc_ref, *, tq, tk, tc, nh, topk, scale):
    step = pl.program_id(0)
    qi, ki = qi_ref[step], ki_ref[step]
    last_ki = (qi * tq + tq - 1) // tk

    @pl.when(ki == 0)
    def _index_and_threshold():
        m_ref[...] = jnp.full(m_ref.shape, NEG, F32)
        l_ref[...] = jnp.zeros(l_ref.shape, F32)
        acc_ref[...] = jnp.zeros(acc_ref.shape, F32)
        for h in range(H_IDX):
            qs_ref[h * tq:(h + 1) * tq, :] = qidx_ref[:, h * D_IDX:(h + 1) * D_IDX]
        n_chunks = (last_ki + 1) * (tk // tc)
        qpos = qi * tq + lax.broadcasted_iota(I32, (tq, tc), 0)

        def score_chunk(c, carry):
            off = pl.multiple_of(c * tc, tc)
            s = lax.dot_general(qs_ref[...], kidx_ref[pl.ds(off, tc), :], NT_DIMS, preferred_element_type=F32)
            score = jnp.zeros((tq, tc), F32)
            for h in range(H_IDX):
                score = score + wi_ref[:, h:h + 1] * jnp.maximum(s[h * tq:(h + 1) * tq, :], 0.0)
            kpos = off + lax.broadcasted_iota(I32, (tq, tc), 1)
            keys_ref[:, pl.ds(off, tc)] = jnp.where(kpos <= qpos, _sortable_key(score), jnp.int32(INT_MIN))
            return carry

        lax.fori_loop(0, n_chunks, score_chunk, 0)

        def count_ge(cand):
            def body(c, acc):
                off = pl.multiple_of(c * tc, tc)
                ge = (keys_ref[:, pl.ds(off, tc)] >= cand).astype(I32)
                for b in range(tc // LANES):
                    acc = acc + ge[:, b * LANES:(b + 1) * LANES]
                return acc
            acc = lax.fori_loop(0, n_chunks, body, jnp.zeros((tq, LANES), I32))
            return jnp.sum(acc, axis=1, keepdims=True)

        thr_ref[...] = _kth_largest_key(count_ge, (tq, 1), topk)

    off_k = pl.multiple_of(ki * tk, tk)
    selected = keys_ref[:, pl.ds(off_k, tk)] >= thr_ref[...]
    for h in range(nh):
        sl = slice(h * HEAD_DIM, (h + 1) * HEAD_DIM)
        s = lax.dot_general(qa_ref[:, sl], ka_ref[:, sl], NT_DIMS, preferred_element_type=F32)
        s = jnp.where(selected, s * scale, NEG)
        _softmax_step(s, h, sl, va_ref, m_ref, l_ref, acc_ref)

    @pl.when(ki == last_ki)
    def _():
        for h in range(nh):
            sl = slice(h * HEAD_DIM, (h + 1) * HEAD_DIM)
            o_ref[:, sl] = (acc_ref[:, sl] / l_ref[h]).astype(o_ref.dtype)


def _dsa_prompt(q_a, k_a, v_a, q_i, k_i, w_i, tq=256, tk=512, tc=256):
    t, w = q_a.shape
    nh = w // HEAD_DIM
    topk = min(TOPK_MAX, t // 4)
    qi_arr, ki_arr = _causal_steps(t // tq, tq, tk)
    grid_spec = pltpu.PrefetchScalarGridSpec(
        num_scalar_prefetch=2,
        grid=(qi_arr.shape[0],),
        in_specs=[
            pl.BlockSpec((tq, w), lambda s, qi, ki: (qi[s], 0)),
            pl.BlockSpec((tk, w), lambda s, qi, ki: (ki[s], 0)),
            pl.BlockSpec((tk, w), lambda s, qi, ki: (ki[s], 0)),
            pl.BlockSpec((tq, H_IDX * D_IDX), lambda s, qi, ki: (qi[s], 0)),
            pl.BlockSpec((t, D_IDX), lambda s, qi, ki: (0, 0)),
            pl.BlockSpec((tq, H_IDX), lambda s, qi, ki: (qi[s], 0)),
        ],
        out_specs=pl.BlockSpec((tq, w), lambda s, qi, ki: (qi[s], 0)),
        scratch_shapes=[
            pltpu.VMEM((tq, t), I32),
            pltpu.VMEM((tq, 1), I32),
            pltpu.VMEM((H_IDX * tq, D_IDX), BF16),
            pltpu.VMEM((nh, tq, 1), F32), pltpu.VMEM((nh, tq, 1), F32), pltpu.VMEM((tq, w), F32),
        ],
    )
    return pl.pallas_call(
        functools.partial(_dsa_kernel, tq=tq, tk=tk, tc=tc, nh=nh, topk=topk, scale=HEAD_DIM ** -0.5),
        grid_spec=grid_spec,
        out_shape=jax.ShapeDtypeStruct((t, w), BF16),
        compiler_params=_cparams("arbitrary"),
        name="dsa_prompt",
    )(qi_arr, ki_arr, q_a, k_a, v_a, q_i, k_i.astype(BF16), w_i)


def _sample_score_kernel(pt_ref, q_ref, w_ref, knew_ref, *rest, n_pages, page):
    kpages = rest[:n_pages]
    key_ref, self_ref = rest[n_pages], rest[n_pages + 1]
    q = q_ref[...]
    w = w_ref[...]

    def key_of(kmat):
        s = lax.dot_general(q, kmat, NT_DIMS, preferred_element_type=F32)
        return _sortable_key(jnp.sum(w * jnp.maximum(s, 0.0), axis=0, keepdims=True))

    for j in range(n_pages):
        key_ref[:, j * page:(j + 1) * page] = key_of(kpages[j][...].astype(BF16))
    knew = jnp.broadcast_to(knew_ref[...].astype(BF16), (8, D_IDX))
    self_ref[...] = jnp.broadcast_to(key_of(knew)[:, 0:1], self_ref.shape)


def _sample_scores(q_i, w_i, k_i_new, cache_kidx, page_table):
    n, n_pages = page_table.shape
    page = cache_kidx.shape[1]
    past = n_pages * page
    pt_flat = page_table.reshape(-1)
    kspecs = [pl.BlockSpec((None, page, D_IDX), functools.partial(lambda i, pt, j: (pt[i * n_pages + j], 0, 0), j=j))
              for j in range(n_pages)]
    grid_spec = pltpu.PrefetchScalarGridSpec(
        num_scalar_prefetch=1,
        grid=(n,),
        in_specs=[pl.BlockSpec((None, H_IDX, D_IDX), lambda i, pt: (i, 0, 0)),
                  pl.BlockSpec((None, H_IDX, 1), lambda i, pt: (i, 0, 0)),
                  pl.BlockSpec((None, 1, D_IDX), lambda i, pt: (i, 0, 0))] + kspecs,
        out_specs=[pl.BlockSpec((None, 1, past), lambda i, pt: (i, 0, 0)),
                   pl.BlockSpec((None, 1, LANES), lambda i, pt: (i, 0, 0))],
    )
    keys, self_key = pl.pallas_call(
        functools.partial(_sample_score_kernel, n_pages=n_pages, page=page),
        grid_spec=grid_spec,
        out_shape=[jax.ShapeDtypeStruct((n, 1, past), I32), jax.ShapeDtypeStruct((n, 1, LANES), I32)],
        compiler_params=_cparams("arbitrary"),
        name="sample_scores",
    )(pt_flat, q_i.reshape(n, H_IDX, D_IDX), w_i.reshape(n, H_IDX, 1), k_i_new.reshape(n, 1, D_IDX),
      *([cache_kidx] * n_pages))
    return keys.reshape(n, past), self_key.reshape(n, LANES)


def _sample_select_kernel(key_ref, self_ref, expand_ref, bias_ref, selfb_ref, *, n_pages, page, nh, topk):
    keys = key_ref[...]
    self_key = self_ref[:, 0:1]

    def count_ge(cand):
        past = jnp.sum((keys >= cand).astype(I32), axis=1, keepdims=True)
        return past + (self_key >= cand).astype(I32)

    thr = _kth_largest_key(count_ge, self_key.shape, topk)
    expand = expand_ref[...]
    rows = page * nh
    for j in range(n_pages):
        sel = jnp.where(keys[:, j * page:(j + 1) * page] >= thr, 1.0, 0.0).astype(BF16)
        sel_x = jnp.dot(sel, expand, preferred_element_type=F32)
        bias_ref[:, j * rows:(j + 1) * rows] = (1.0 - sel_x) * NEG
    selfb_ref[...] = jnp.broadcast_to(jnp.where(self_key >= thr, 0.0, NEG), selfb_ref.shape)


def _sample_select(keys, self_key, page, nh):
    n, past = keys.shape
    n_pages = past // page
    topk = min(TOPK_MAX, (past + 1) // 4)
    expand = np.zeros((page, page * nh), np.float32)
    for k in range(page):
        expand[k, k * nh:(k + 1) * nh] = 1.0
    return pl.pallas_call(
        functools.partial(_sample_select_kernel, n_pages=n_pages, page=page, nh=nh, topk=topk),
        out_shape=[jax.ShapeDtypeStruct((n, past * nh), F32), jax.ShapeDtypeStruct((n, LANES), F32)],
        compiler_params=_cparams(),
        name="sample_select",
    )(keys, self_key, jnp.asarray(expand, BF16))


def _split3(x):
    hi = x.astype(BF16)
    r1 = x - hi.astype(F32)
    mid = r1.astype(BF16)
    lo = (r1 - mid.astype(F32)).astype(BF16)
    return hi, mid, lo


def _sample_attn_kernel(pt_ref, q_ref, knew_ref, vnew_ref, selfb_ref, *rest, pps, nh, scale, fox):
    kp = rest[:pps]
    vp = rest[pps:2 * pps]
    if fox:
        lfp, pre_ref = rest[2 * pps:3 * pps], rest[3 * pps]
        o_ref, m_ref, l_ref, acc_ref, carry_ref = rest[3 * pps + 1:]
    else:
        bias_ref = rest[2 * pps]
        o_ref, m_ref, l_ref, acc_ref = rest[2 * pps + 1:]
    p_step = pl.program_id(1)
    rows = kp[0].shape[0]
    sub = lax.broadcasted_iota(I32, (nh, rows), 0)
    lane = lax.broadcasted_iota(I32, (nh, rows), 1)
    own_head = lax.rem(lane, nh) == sub
    q = q_ref[...]

    @pl.when(p_step == 0)
    def _():
        m_ref[...] = jnp.full(m_ref.shape, NEG, F32)
        l_ref[...] = jnp.zeros(l_ref.shape, F32)
        acc_ref[...] = jnp.zeros(acc_ref.shape, F32)
        if fox:
            carry_ref[...] = jnp.zeros(carry_ref.shape, F32)

    def update(s, vmat):
        m_prev = m_ref[...]
        m_new = jnp.maximum(m_prev, jnp.max(s, axis=1, keepdims=True))
        alpha = jnp.exp(m_prev - m_new)
        p = jnp.exp(s - m_new)
        l_ref[...] = alpha * l_ref[...] + jnp.sum(p, axis=1, keepdims=True)
        acc_ref[...] = alpha * acc_ref[...] + jnp.dot(p.astype(BF16), vmat, preferred_element_type=F32)
        m_ref[...] = m_new

    for j in range(pps):
        s = lax.dot_general(q, kp[j][...].astype(BF16), NT_DIMS, preferred_element_type=F32) * scale
        if fox:
            lf = lfp[j][...]
            pre = carry_ref[...]
            for part in _split3(lf):
                pre = pre + jnp.dot(part, pre_ref[...], preferred_element_type=F32)
            carry_ref[...] = carry_ref[...] + jnp.sum(lf, axis=1, keepdims=True)
            s = s - pre
        else:
            s = s + bias_ref[j]
        update(jnp.where(own_head, s, NEG), vp[j][...].astype(BF16))

    @pl.when(p_step == pl.num_programs(1) - 1)
    def _():
        knew = knew_ref[...].astype(BF16).astype(F32)
        s_self = jnp.sum(q.astype(F32) * knew, axis=1, keepdims=True) * scale
        if fox:
            s_self = s_self - (carry_ref[...] + selfb_ref[...])
        else:
            s_self = s_self + selfb_ref[...]
        m_prev = m_ref[...]
        m_new = jnp.maximum(m_prev, s_self)
        alpha = jnp.exp(m_prev - m_new)
        p = jnp.exp(s_self - m_new)
        l_fin = alpha * l_ref[...] + p
        vnew = vnew_ref[...].astype(BF16).astype(F32)
        acc = alpha * acc_ref[...] + p.astype(BF16).astype(F32) * vnew
        o_ref[...] = (acc / l_fin).astype(o_ref.dtype)


def _sample_attn(q, k_new, v_new, cache_k, cache_v, page_table, selfb, extra, fox, pps=4):
    n, nh, hd = q.shape
    n_pages = page_table.shape[1]
    rows = cache_k.shape[1]
    page = rows // nh
    pt_flat = page_table.reshape(-1)

    def page_spec(j, shape):
        return pl.BlockSpec((None,) + shape, lambda i, p, pt: (pt[i * n_pages + p * pps + j], 0, 0))

    per_seq = lambda shape: pl.BlockSpec((None,) + shape, lambda i, p, pt: (i, 0, 0))
    in_specs = [per_seq((nh, hd))] * 3 + [per_seq((nh, 1))]
    in_specs += [page_spec(j, (rows, hd)) for j in range(pps)] * 2
    args = [q, k_new, v_new, selfb] + [cache_k] * pps + [cache_v] * pps
    scratch = [pltpu.VMEM((nh, 1), F32), pltpu.VMEM((nh, 1), F32), pltpu.VMEM((nh, hd), F32)]
    if fox:
        cache_lf_t = extra
        prefix_expand = np.zeros((page, rows), np.float32)
        for k in range(page):
            prefix_expand[k, k * nh:] = 1.0
        in_specs += [page_spec(j, (nh, page)) for j in range(pps)]
        in_specs += [pl.BlockSpec((page, rows), lambda i, p, pt: (0, 0))]
        args += [cache_lf_t] * pps + [jnp.asarray(prefix_expand, BF16)]
        scratch += [pltpu.VMEM((nh, 1), F32)]
    else:
        bias = extra
        in_specs += [pl.BlockSpec((None, pps, 1, rows), lambda i, p, pt: (i, p, 0, 0))]
        args += [bias]
    grid_spec = pltpu.PrefetchScalarGridSpec(
        num_scalar_prefetch=1,
        grid=(n, n_pages // pps),
        in_specs=in_specs,
        out_specs=per_seq((nh, hd)),
        scratch_shapes=scratch,
    )
    return pl.pallas_call(
        functools.partial(_sample_attn_kernel, pps=pps, nh=nh, scale=HEAD_DIM ** -0.5, fox=fox),
        grid_spec=grid_spec,
        out_shape=jax.ShapeDtypeStruct((n, nh, hd), BF16),
        compiler_params=_cparams("arbitrary", "arbitrary"),
        name="sample_fox" if fox else "sample_dsa",
    )(pt_flat, *args)


def _merge_kernel(x_ref, oa_ref, ob_ref, g_ref, wao_ref, wbo_ref, wout_ref, gffn_ref, wr_ref, hn_in_ref,
                  h_ref, hn_ref, eid_ref, gate_ref):
    del hn_in_ref
    d = x_ref.shape[1]
    br_a = jnp.dot(oa_ref[...], wao_ref[...], preferred_element_type=F32)
    br_b = jnp.dot(ob_ref[...], wbo_ref[...], preferred_element_type=F32)
    mix = g_ref[:, :d] * br_a + g_ref[:, d:] * br_b
    h = x_ref[...] + jnp.dot(mix.astype(BF16), wout_ref[...], preferred_element_type=F32)
    h_ref[...] = h
    hn = h * lax.rsqrt(jnp.mean(h * h, axis=-1, keepdims=True) + EPS) * gffn_ref[...]
    hn_ref[...] = hn

    logits = lax.dot_general(wr_ref[...], hn, NT_DIMS, preferred_element_type=F32,
                             precision=lax.Precision.HIGHEST)
    tm = logits.shape[1]
    gl = logits[0:N_GROUPS]
    row_g = lax.broadcasted_iota(I32, gl.shape, 0)
    gmax = jnp.max(gl, axis=0, keepdims=True)
    g_idx = jnp.min(jnp.where(gl == gmax, row_g, N_GROUPS), axis=0, keepdims=True)
    g_w = 1.0 / jnp.sum(jnp.exp(gl - gmax), axis=0, keepdims=True)
    in_l = jnp.zeros((EXPERTS_PER_GROUP, tm), F32)
    for g in range(N_GROUPS):
        lo = 8 + g * EXPERTS_PER_GROUP
        in_l = in_l + jnp.where(g_idx == g, logits[lo:lo + EXPERTS_PER_GROUP], 0.0)
    e = jnp.exp(in_l - jnp.max(in_l, axis=0, keepdims=True))
    in_p = e / jnp.sum(e, axis=0, keepdims=True)
    row_e = lax.broadcasted_iota(I32, in_p.shape, 0)
    p1 = jnp.max(in_p, axis=0, keepdims=True)
    i1 = jnp.min(jnp.where(in_p == p1, row_e, EXPERTS_PER_GROUP), axis=0, keepdims=True)
    rest = jnp.where(row_e == i1, -1.0, in_p)
    p2 = jnp.max(rest, axis=0, keepdims=True)
    i2 = jnp.min(jnp.where(rest == p2, row_e, EXPERTS_PER_GROUP), axis=0, keepdims=True)
    denom = p1 + p2
    row8 = lax.broadcasted_iota(I32, (8, tm), 0)
    eid_ref[...] = jnp.where(row8 == 0, g_idx * EXPERTS_PER_GROUP + i1,
                             jnp.where(row8 == 1, g_idx * EXPERTS_PER_GROUP + i2, 0))
    gate_ref[...] = jnp.where(row8 == 0, p1 / denom * g_w, jnp.where(row8 == 1, p2 / denom * g_w, 0.0))


def _merge(x2d, o_a, o_b, gates, w_a_o, w_b_o, w_out, g_ffn, wr_t, hn_all, row_off, tm=256):
    m, d = x2d.shape
    tm = min(tm, m)
    assert row_off % tm == 0
    blk_off = row_off // tm
    wa = o_a.shape[1]
    const = lambda shape: pl.BlockSpec(shape, lambda i: (0, 0), pipeline_mode=pl.Buffered(1))
    return pl.pallas_call(
        _merge_kernel,
        grid=(m // tm,),
        in_specs=[pl.BlockSpec((tm, d), lambda i: (i, 0)),
                  pl.BlockSpec((tm, wa), lambda i: (i, 0)),
                  pl.BlockSpec((tm, wa), lambda i: (i, 0)),
                  pl.BlockSpec((tm, 2 * d), lambda i: (i, 0)),
                  const((wa, d)), const((wa, d)), const((d, d)), const((1, d)), const(wr_t.shape),
                  pl.BlockSpec(memory_space=pl.ANY)],
        out_specs=[pl.BlockSpec((tm, d), lambda i: (i, 0)), pl.BlockSpec((tm, d), lambda i: (i + blk_off, 0)),
                   pl.BlockSpec((8, tm), lambda i: (0, i)), pl.BlockSpec((8, tm), lambda i: (0, i))],
        out_shape=[jax.ShapeDtypeStruct((m, d), F32), jax.ShapeDtypeStruct(hn_all.shape, F32),
                   jax.ShapeDtypeStruct((8, m), I32), jax.ShapeDtypeStruct((8, m), F32)],
        input_output_aliases={9: 1},
        compiler_params=_cparams("parallel"),
        name="merge_router",
    )(x2d, o_a, o_b, gates, w_a_o, w_b_o, w_out, g_ffn.reshape(1, d), wr_t, hn_all)


def _row_token_kernel(dest_ref, rt_ref, *, n_assign, m_tot, n_rows):
    def init(r, carry):
        rt_ref[r] = 0
        return carry

    def scatter(a, carry):
        rt_ref[dest_ref[a]] = jnp.where(a >= m_tot, a - m_tot, a)
        return carry

    lax.fori_loop(0, n_rows, init, 0, unroll=8)
    lax.fori_loop(0, n_assign, scatter, 0, unroll=8)


def _row_tokens(dest, m_tot, n_rows):
    n_assign = dest.shape[0]
    grid_spec = pltpu.PrefetchScalarGridSpec(
        num_scalar_prefetch=1,
        grid=(1,),
        in_specs=[],
        out_specs=pl.BlockSpec(memory_space=pltpu.SMEM),
    )
    return pl.pallas_call(
        functools.partial(_row_token_kernel, n_assign=n_assign, m_tot=m_tot, n_rows=n_rows),
        grid_spec=grid_spec,
        out_shape=jax.ShapeDtypeStruct((n_rows,), I32),
        compiler_params=_cparams("arbitrary"),
        name="moe_row_tokens",
    )(dest)


def _expert_kernel(te_ref, nv_ref, rt_ref, hn_ref, w1_ref, w3_ref, wd_ref, y_ref, xbuf, w1_s, w3_s, wd_s, sems, *, tm):
    i = pl.program_id(0)
    n_valid = nv_ref[0]

    def row_copy(tile, slot, r):
        tok = rt_ref[tile * tm + r]
        return pltpu.make_async_copy(hn_ref.at[pl.ds(tok, 1)], xbuf.at[slot, pl.ds(r, 1)], sems.at[slot])

    def start_gather(tile, slot):
        def body(r, carry):
            row_copy(tile, slot, r).start()
            return carry
        lax.fori_loop(0, tm, body, 0, unroll=8)

    def wait_gather(tile, slot):
        def body(r, carry):
            row_copy(tile, slot, r).wait()
            return carry
        lax.fori_loop(0, tm, body, 0, unroll=8)

    @pl.when(i == 0)
    def _():
        start_gather(0, 0)

    @pl.when(i + 1 < n_valid)
    def _():
        start_gather(i + 1, (i + 1) % 2)

    changed = jnp.logical_or(i == 0, te_ref[i] != te_ref[jnp.maximum(i - 1, 0)])

    @pl.when(changed)
    def _():
        w1_s[...] = w1_ref[...].astype(BF16)
        w3_s[...] = w3_ref[...].astype(BF16)
        wd_s[...] = wd_ref[...].astype(BF16)

    @pl.when(i < n_valid)
    def _():
        slot = i % 2
        wait_gather(i, slot)
        x = xbuf[slot].astype(BF16)
        u1 = jnp.dot(x, w1_s[...], preferred_element_type=F32)
        u3 = jnp.dot(x, w3_s[...], preferred_element_type=F32)
        hidden = u1 / (1.0 + jnp.exp(-u1)) * u3
        y_ref[...] = jnp.dot(hidden.astype(BF16), wd_s[...], preferred_element_type=F32)

    @pl.when(i >= n_valid)
    def _():
        y_ref[...] = jnp.zeros(y_ref.shape, F32)


def _experts(tile_expert, n_valid, row_token, hn_all, w_up1, w_up3, w_down, tm):
    n_rows = row_token.shape[0]
    d = hn_all.shape[1]
    f = w_up1.shape[2]
    grid_spec = pltpu.PrefetchScalarGridSpec(
        num_scalar_prefetch=3,
        grid=(n_rows // tm,),
        in_specs=[pl.BlockSpec(memory_space=pl.ANY),
                  pl.BlockSpec((None, d, f), lambda i, te, nv, rt: (te[i], 0, 0)),
                  pl.BlockSpec((None, d, f), lambda i, te, nv, rt: (te[i], 0, 0)),
                  pl.BlockSpec((None, f, d), lambda i, te, nv, rt: (te[i], 0, 0))],
        out_specs=pl.BlockSpec((tm, d), lambda i, te, nv, rt: (i, 0)),
        scratch_shapes=[pltpu.VMEM((2, tm, d), F32),
                        pltpu.VMEM((d, f), BF16), pltpu.VMEM((d, f), BF16), pltpu.VMEM((f, d), BF16),
                        pltpu.SemaphoreType.DMA((2,))],
    )
    return pl.pallas_call(
        functools.partial(_expert_kernel, tm=tm),
        grid_spec=grid_spec,
        out_shape=jax.ShapeDtypeStruct((n_rows, d), F32),
        compiler_params=_cparams("arbitrary"),
        name="moe_experts",
    )(tile_expert, n_valid, row_token, hn_all, w_up1, w_up3, w_down)


def _combine_kernel(dest_ref, h_ref, gate_ref, gfin_ref, ys_ref, y_ref, buf0, buf1, sem, *, tok_off, m_tot, tm):
    base = tok_off + pl.program_id(0) * tm

    def copies(r):
        return (pltpu.make_async_copy(ys_ref.at[pl.ds(dest_ref[base + r], 1)], buf0.at[pl.ds(r, 1)], sem),
                pltpu.make_async_copy(ys_ref.at[pl.ds(dest_ref[m_tot + base + r], 1)], buf1.at[pl.ds(r, 1)], sem))

    def start(r, c):
        for cp in copies(r):
            cp.start()
        return c

    def wait(r, c):
        for cp in copies(r):
            cp.wait()
        return c

    lax.fori_loop(0, tm, start, 0)
    lax.fori_loop(0, tm, wait, 0)
    out = h_ref[...] + gate_ref[:, 0:1] * buf0[...] + gate_ref[:, 1:2] * buf1[...]
    y = out * lax.rsqrt(jnp.mean(out * out, axis=-1, keepdims=True) + EPS)
    y_ref[...] = y * gfin_ref[...]


def _combine(dest, h, gate_cols, g_final, ys, tok_off, m_tot, tm=128):
    m, d = h.shape
    tm = min(tm, m)
    grid_spec = pltpu.PrefetchScalarGridSpec(
        num_scalar_prefetch=1,
        grid=(m // tm,),
        in_specs=[pl.BlockSpec((tm, d), lambda i, dr: (i, 0)),
                  pl.BlockSpec((tm, 2), lambda i, dr: (i, 0)),
                  pl.BlockSpec((1, d), lambda i, dr: (0, 0)),
                  pl.BlockSpec(memory_space=pl.ANY)],
        out_specs=pl.BlockSpec((tm, d), lambda i, dr: (i, 0)),
        scratch_shapes=[pltpu.VMEM((tm, d), F32), pltpu.VMEM((tm, d), F32), pltpu.SemaphoreType.DMA(())],
    )
    return pl.pallas_call(
        functools.partial(_combine_kernel, tok_off=tok_off, m_tot=m_tot, tm=tm),
        grid_spec=grid_spec,
        out_shape=jax.ShapeDtypeStruct((m, d), F32),
        compiler_params=_cparams("arbitrary"),
        name="moe_combine",
    )(dest, h, gate_cols, g_final.reshape(1, d), ys)


def _moe_plan(eids, tm):
    e_flat = eids.reshape(-1)
    n_assign = e_flat.shape[0]
    onehot = (e_flat[:, None] == jnp.arange(N_EXPERTS, dtype=I32)[None, :]).astype(I32)
    csum = jnp.cumsum(onehot, axis=0)
    rank = jnp.sum((csum - onehot) * onehot, axis=1)
    counts = csum[-1]
    tiles_e = (counts + tm - 1) // tm
    tile_end = jnp.cumsum(tiles_e)
    dest = ((tile_end - tiles_e) * tm)[e_flat] + rank
    n_tiles = n_assign // tm + N_EXPERTS
    n_valid = tile_end[-1]
    tile_ids = jnp.minimum(jnp.arange(n_tiles, dtype=I32), n_valid - 1)
    tile_expert = jnp.sum((tile_end[None, :] <= tile_ids[:, None]).astype(I32), axis=1)
    return dest.astype(I32), tile_expert.astype(I32), n_valid.reshape(1).astype(I32), n_tiles


def kernel(x_prompt, x_sample, cache_k_a, cache_v_a, cache_kidx_a, cache_k_b, cache_v_b, cache_logf_b,
           page_table, g_mix, w_in, f_bias, w_a_o, w_b_o, w_out, g_ffn, w_grp, w_exp, w_up1, w_up3,
           w_down, g_final):
    depth = w_in.shape[0]
    assert depth == 1, "single-layer stack"
    batch, seq, d = x_prompt.shape
    n_dec, dec_seq, _ = x_sample.shape
    assert batch == 1 and dec_seq == 1
    n_pool, page = cache_k_a.shape[1], cache_k_a.shape[2]
    n_pages = page_table.shape[1]
    past = n_pages * page
    l = 0

    pp = _project_group(x_prompt.reshape(seq, d), jnp.arange(seq), g_mix[l], w_in[l], f_bias[l])
    ps = _project_group(x_sample.reshape(n_dec, d), jnp.full((n_dec,), past, I32), g_mix[l], w_in[l], f_bias[l])
    w_a = pp["q_a"].shape[1]
    h_a = h_b = w_a // HEAD_DIM

    o_a_p = _dsa_prompt(pp["q_a"], pp["k_a"], pp["v_a"], pp["q_i"], pp["k_i"], pp["w_i"])
    ck_t = _cumsum_lanes(pp["log_f"].T)
    o_b_p = _fox_prompt(pp["q_b"], pp["k_b"], pp["v_b"], ck_t)

    paged = lambda c: c[l].reshape(n_pool, page * h_a, HEAD_DIM)
    per_head = lambda a: a.reshape(n_dec, h_a, HEAD_DIM)
    keys_s, self_key = _sample_scores(ps["q_i"], ps["w_i"], ps["k_i"], cache_kidx_a[l], page_table)
    bias_s, selfb = _sample_select(keys_s, self_key, page, h_a)
    selfb_a = jnp.broadcast_to(selfb[:, 0:1, None], (n_dec, h_a, 1))
    o_a_s = _sample_attn(per_head(ps["q_a"]), per_head(ps["k_a32"]), per_head(ps["v_a32"]),
                         paged(cache_k_a), paged(cache_v_a), page_table, selfb_a,
                         bias_s.reshape(n_dec, n_pages, 1, page * h_a), fox=False).reshape(n_dec, w_a)
    cache_lf_t = jnp.swapaxes(cache_logf_b[l], 1, 2)
    o_b_s = _sample_attn(per_head(ps["q_b"]), per_head(ps["k_b32"]), per_head(ps["v_b32"]),
                         paged(cache_k_b), paged(cache_v_b), page_table, ps["log_f"].reshape(n_dec, h_b, 1),
                         cache_lf_t, fox=True).reshape(n_dec, w_a)

    m_tot = seq + n_dec
    wr_t = jnp.zeros((8 + N_EXPERTS, d), F32).at[:N_GROUPS].set(w_grp[l].T).at[8:].set(w_exp[l].T)
    wao, wbo, wo = w_a_o[l].astype(BF16), w_b_o[l].astype(BF16), w_out[l].astype(BF16)
    hn_all = jnp.zeros((m_tot, d), F32)
    h_s, hn_all, eid_s, gate_s = _merge(x_sample.reshape(n_dec, d), o_a_s, o_b_s, ps["gates"], wao, wbo, wo,
                                        g_ffn[l], wr_t, hn_all, seq)
    h_p, hn_all, eid_p, gate_p = _merge(x_prompt.reshape(seq, d), o_a_p, o_b_p, pp["gates"], wao, wbo, wo,
                                        g_ffn[l], wr_t, hn_all, 0)

    tm_e = 256
    eids = jnp.concatenate([eid_p[:2], eid_s[:2]], axis=1)
    dest, tile_expert, n_valid, n_tiles = _moe_plan(eids, tm_e)
    row_token = _row_tokens(dest, m_tot, n_tiles * tm_e)
    ys = _experts(tile_expert, n_valid, row_token, hn_all, w_up1[l], w_up3[l], w_down[l], tm_e)
    y_p = _combine(dest, h_p, gate_p[:2].T, g_final, ys, 0, m_tot)
    y_s = _combine(dest, h_s, gate_s[:2].T, g_final, ys, seq, m_tot)

    kv = lambda a, b, t, nh: a.reshape(depth, b, t, nh, HEAD_DIM)
    return (y_p.reshape(batch, seq, d), y_s.reshape(n_dec, dec_seq, d),
            kv(pp["k_a32"], batch, seq, h_a), kv(pp["v_a32"], batch, seq, h_a),
            pp["k_i"].reshape(depth, batch, seq, D_IDX),
            kv(pp["k_b32"], batch, seq, h_b), kv(pp["v_b32"], batch, seq, h_b),
            pp["log_f"].reshape(depth, batch, seq, h_b),
            kv(ps["k_a32"], n_dec, dec_seq, h_a), kv(ps["v_a32"], n_dec, dec_seq, h_a),
            ps["k_i"].reshape(depth, n_dec, dec_seq, D_IDX),
            kv(ps["k_b32"], n_dec, dec_seq, h_b), kv(ps["v_b32"], n_dec, dec_seq, h_b),
            ps["log_f"].reshape(depth, n_dec, dec_seq, h_b))
```

```python
import functools

import numpy as np
import jax
import jax.numpy as jnp
from jax import lax
from jax.experimental import pallas as pl
from jax.experimental.pallas import tpu as pltpu

F32 = jnp.float32
BF16 = jnp.bfloat16
I32 = jnp.int32

HEAD_DIM = 128
H_IDX = 16
D_IDX = 64
TOPK_MAX = 256
ROT_A = HEAD_DIM // 4
ROT_IDX = D_IDX // 4
ROPE_THETA = 500000.0
N_GROUPS = 4
EXPERTS_PER_GROUP = 8
N_EXPERTS = N_GROUPS * EXPERTS_PER_GROUP
EPS = 1e-6

LANES = 128
NEG = -1e30
INT_MIN = -(2 ** 31)
VMEM_LIMIT = 56 * 1024 * 1024

NT_DIMS = (((1,), (1,)), ((), ()))


def _cparams(*sem, vmem=VMEM_LIMIT):
    return pltpu.CompilerParams(dimension_semantics=sem, vmem_limit_bytes=vmem)


def _rmsnorm_kernel(x_ref, g_ref, o_ref):
    x = x_ref[...]
    y = x * lax.rsqrt(jnp.mean(x * x, axis=-1, keepdims=True) + EPS)
    o_ref[...] = (y * g_ref[...]).astype(o_ref.dtype)


def _rmsnorm(x, g, out_dtype, tm):
    m, d = x.shape
    return pl.pallas_call(
        _rmsnorm_kernel,
        grid=(m // tm,),
        in_specs=[pl.BlockSpec((tm, d), lambda i: (i, 0)), pl.BlockSpec((1, d), lambda i: (0, 0))],
        out_specs=pl.BlockSpec((tm, d), lambda i: (i, 0)),
        out_shape=jax.ShapeDtypeStruct((m, d), out_dtype),
        compiler_params=_cparams("parallel"),
        name="rmsnorm",
    )(x, g.reshape(1, d))


def _rope_tables(pos, rot_dim, period, active_lanes):
    half = rot_dim // 2
    inv_freq = ROPE_THETA ** (-jnp.arange(half, dtype=F32) / half)
    ang = pos.astype(F32)[:, None] * inv_freq[None, :]
    cos, sin = jnp.cos(ang), jnp.sin(ang)
    lane = np.arange(LANES)
    d = lane % period
    first = (d < half) & (lane < active_lanes)
    second = (d >= half) & (d < rot_dim) & (lane < active_lanes)
    idx = np.where(first, d, np.where(second, d - half, 0))
    cos_l, sin_l = cos[:, idx], sin[:, idx]
    rot = jnp.asarray(first | second)[None, :]
    c = jnp.where(rot, cos_l, 1.0)
    s1 = jnp.where(jnp.asarray(first)[None, :], -sin_l, 0.0)
    s2 = jnp.where(jnp.asarray(second)[None, :], sin_l, 0.0)
    return c, s1, s2


def _rope_lanes(y, c, s1, s2, half):
    return y * c + pltpu.roll(y, LANES - half, 1) * s1 + pltpu.roll(y, half, 1) * s2


def _proj_kernel(*refs, mode, half, n_out, out_scale):
    x_ref, w_ref = refs[0], refs[1]
    outs = refs[len(refs) - n_out:]
    acc = jnp.dot(x_ref[...], w_ref[...], preferred_element_type=F32)
    tn = acc.shape[1]
    if mode == "rope":
        c, s1, s2 = refs[2][...], refs[3][...], refs[4][...]
        for b in range(tn // LANES):
            sl = slice(b * LANES, (b + 1) * LANES)
            y = _rope_lanes(acc[:, sl], c, s1, s2, half)
            if out_scale != 1.0:
                y = y * out_scale
            for o in outs:
                o[:, sl] = y.astype(o.dtype)
        return
    if mode == "sigmoid":
        acc = 1.0 / (1.0 + jnp.exp(-acc))
    if out_scale != 1.0:
        acc = acc * out_scale
    for o in outs:
        o[...] = acc.astype(o.dtype)


def _proj_t_kernel(x_ref, wt_ref, o_ref):
    o_ref[...] = lax.dot_general(wt_ref[...], x_ref[...], NT_DIMS, preferred_element_type=F32).astype(o_ref.dtype)


def _proj_t(xn, w_t, tm=512):
    m, d = xn.shape
    n = w_t.shape[0]
    tm = min(tm, m)
    return pl.pallas_call(
        _proj_t_kernel,
        grid=(m // tm,),
        in_specs=[pl.BlockSpec((tm, d), lambda i: (i, 0)), pl.BlockSpec((n, d), lambda i: (0, 0))],
        out_specs=pl.BlockSpec((n, tm), lambda i: (0, i)),
        out_shape=jax.ShapeDtypeStruct((n, m), BF16),
        compiler_params=_cparams("parallel"),
        name="proj_t",
    )(xn, w_t)


def _proj(xn, w, out_dtypes, mode="none", tables=None, half=0, tm=512, tn=1024, out_scale=1.0):
    m, d = xn.shape
    n = w.shape[1]
    tm = min(tm, m)
    in_specs = [pl.BlockSpec((tm, d), lambda j, i: (i, 0)), pl.BlockSpec((d, tn), lambda j, i: (0, j))]
    args = [xn, w]
    if mode == "rope":
        in_specs += [pl.BlockSpec((tm, LANES), lambda j, i: (i, 0))] * 3
        args += list(tables)
    outs = pl.pallas_call(
        functools.partial(_proj_kernel, mode=mode, half=half, n_out=len(out_dtypes), out_scale=out_scale),
        grid=(n // tn, m // tm),
        in_specs=in_specs,
        out_specs=[pl.BlockSpec((tm, tn), lambda j, i: (i, j)) for _ in out_dtypes],
        out_shape=[jax.ShapeDtypeStruct((m, n), dt) for dt in out_dtypes],
        compiler_params=_cparams("parallel", "parallel"),
        name="proj_" + mode,
    )(*args)
    return outs


def _proj_small_kernel(x_ref, w_ref, c_ref, s1_ref, s2_ref, bias_ref, o_ref, *, half, wi_scale):
    acc = jnp.dot(x_ref[...], w_ref[...], preferred_element_type=F32)
    roped = _rope_lanes(acc, c_ref[...], s1_ref[...], s2_ref[...], half)
    z = acc + bias_ref[...]
    logsig = -(jnp.maximum(-z, 0.0) + jnp.log(1.0 + jnp.exp(-jnp.abs(z))))
    lane = lax.broadcasted_iota(I32, acc.shape, 1)
    o_ref[...] = jnp.where(lane < D_IDX, roped, jnp.where(lane < D_IDX + H_IDX, acc * wi_scale, logsig))


def _proj_small(xn, w_small, tables, bias_row, tm=512):
    m, d = xn.shape
    tm = min(tm, m)
    wi_scale = (H_IDX ** -0.5) * (D_IDX ** -0.5)
    return pl.pallas_call(
        functools.partial(_proj_small_kernel, half=ROT_IDX // 2, wi_scale=wi_scale),
        grid=(m // tm,),
        in_specs=[pl.BlockSpec((tm, d), lambda i: (i, 0)), pl.BlockSpec((d, LANES), lambda i: (0, 0))]
        + [pl.BlockSpec((tm, LANES), lambda i: (i, 0))] * 3 + [pl.BlockSpec((1, LANES), lambda i: (0, 0))],
        out_specs=pl.BlockSpec((tm, LANES), lambda i: (i, 0)),
        out_shape=jax.ShapeDtypeStruct((m, LANES), F32),
        compiler_params=_cparams("parallel"),
        name="proj_small",
    )(xn, w_small, *tables, bias_row)


def _project_group(x2d, pos, g_mix, w_in, f_bias, feature_major_v):
    m, d = x2d.shape
    h_a = h_b = (w_in.shape[1] - (H_IDX * D_IDX + D_IDX + H_IDX) - 2 * d) // (6 * HEAD_DIM + 1)
    w_a = h_a * HEAD_DIM
    sizes = (w_a, w_a, w_a, H_IDX * D_IDX, D_IDX, H_IDX, w_a, w_a, w_a, h_b, d, d)
    offs = np.concatenate([[0], np.cumsum(sizes)])
    seg = lambda k: w_in[:, offs[k]:offs[k + 1]].astype(BF16)

    xn = _rmsnorm(x2d, g_mix, BF16, tm=min(512, m))
    tab_a = _rope_tables(pos, ROT_A, HEAD_DIM, LANES)
    tab_i = _rope_tables(pos, ROT_IDX, D_IDX, LANES)
    tab_k = _rope_tables(pos, ROT_IDX, D_IDX, ROT_IDX)

    qk_scale = HEAD_DIM ** -0.5
    (q_a,) = _proj(xn, seg(0), [BF16], "rope", tab_a, ROT_A // 2, out_scale=qk_scale)
    k_a32, k_a = _proj(xn, seg(1), [F32, BF16], "rope", tab_a, ROT_A // 2)
    (v_a32,) = _proj(xn, seg(2), [F32])
    (q_i,) = _proj(xn, seg(3), [BF16], "rope", tab_i, ROT_IDX // 2)
    (q_b,) = _proj(xn, seg(6), [BF16], out_scale=qk_scale)
    k_b32, k_b = _proj(xn, seg(7), [F32, BF16])
    (v_b32,) = _proj(xn, seg(8), [F32])
    (gates,) = _proj(xn, w_in[:, offs[10]:offs[12]].astype(BF16), [F32], "sigmoid")

    n_small = D_IDX + H_IDX + h_b
    w_small = jnp.concatenate([w_in[:, offs[4]:offs[6]], w_in[:, offs[9]:offs[10]],
                               jnp.zeros((d, LANES - n_small), F32)], axis=1).astype(BF16)
    bias_row = jnp.zeros((1, LANES), F32).at[0, D_IDX + H_IDX:n_small].set(f_bias)
    small = _proj_small(xn, w_small, tab_k, bias_row)
    k_i = small[:, :D_IDX]
    w_i = small[:, D_IDX:D_IDX + H_IDX]
    log_f = small[:, D_IDX + H_IDX:n_small]
    out = dict(q_a=q_a, k_a=k_a, k_a32=k_a32, v_a32=v_a32, q_i=q_i, k_i=k_i, w_i=w_i,
               q_b=q_b, k_b=k_b, k_b32=k_b32, v_b32=v_b32, log_f=log_f, gates=gates)
    if feature_major_v:
        out["v_a_t"] = _proj_t(xn, seg(2).T)
        out["v_b_t"] = _proj_t(xn, seg(8).T)
    return out


def _split3(x):
    hi = x.astype(BF16).astype(F32)
    r1 = x - hi
    mid = r1.astype(BF16).astype(F32)
    lo = (r1 - mid).astype(BF16).astype(F32)
    return hi, mid, lo


def _cumsum_kernel(x_ref, o_ref, *, blk):
    n = x_ref.shape[1]
    r = lax.broadcasted_iota(I32, (blk, blk), 0)
    c = lax.broadcasted_iota(I32, (blk, blk), 1)
    tri = (r <= c).astype(F32)

    def body(i, carry):
        off = pl.multiple_of(i * blk, blk)
        xb = x_ref[:, pl.ds(off, blk)]
        cs = jnp.dot(xb, tri, preferred_element_type=F32, precision=lax.Precision.HIGHEST) + carry
        for j, part in enumerate(_split3(cs)):
            o_ref[j, :, pl.ds(off, blk)] = part
        return cs[:, blk - 1:blk]

    lax.fori_loop(0, n // blk, body, jnp.zeros((x_ref.shape[0], 1), F32))


def _cumsum_parts(x_t, blk=LANES):
    return pl.pallas_call(
        functools.partial(_cumsum_kernel, blk=blk),
        out_shape=jax.ShapeDtypeStruct((3,) + x_t.shape, F32),
        compiler_params=_cparams(),
        name="cumsum_logf",
    )(x_t)


def _causal_steps(nq, tq, tk):
    qi_l, ki_l = [], []
    for qi in range(nq):
        for ki in range((qi * tq + tq - 1) // tk + 1):
            qi_l.append(qi)
            ki_l.append(ki)
    return jnp.asarray(np.array(qi_l, np.int32)), jnp.asarray(np.array(ki_l, np.int32))


def _attn_head_t(kx, qx, vt_h, mask, h, m_ref, l_ref, acc_ref):
    s = lax.dot_general(kx, qx, NT_DIMS, preferred_element_type=F32)
    if mask is not None:
        s = jnp.where(mask, s, NEG)
    m_prev = m_ref[h]
    m_new = jnp.maximum(m_prev, jnp.max(s, axis=0, keepdims=True))
    alpha = jnp.exp(m_prev - m_new)
    p = jnp.exp(s - m_new)
    l_ref[h] = alpha * l_ref[h] + jnp.sum(p, axis=0, keepdims=True)
    acc_ref[h] = alpha * acc_ref[h] + jnp.dot(vt_h, p.astype(BF16), preferred_element_type=F32)
    m_ref[h] = m_new


def _attn_init(m_ref, l_ref, acc_ref):
    m_ref[...] = jnp.full(m_ref.shape, NEG, F32)
    l_ref[...] = jnp.zeros(l_ref.shape, F32)
    acc_ref[...] = jnp.zeros(acc_ref.shape, F32)


def _attn_finish(o_ref, l_ref, acc_ref, nh):
    for h in range(nh):
        o_ref[:, h * HEAD_DIM:(h + 1) * HEAD_DIM] = (acc_ref[h] / l_ref[h]).T.astype(o_ref.dtype)


def _attn_scratch(nh, tq):
    return [pltpu.VMEM((nh, 1, tq), F32), pltpu.VMEM((nh, 1, tq), F32), pltpu.VMEM((nh, HEAD_DIM, tq), F32)]


def _fox_kernel(qi_ref, ki_ref, q_ref, k_ref, kb_ref, vt_ref, ones_ref, o_ref, m_ref, l_ref, acc_ref, *, tq, tk, nh):
    step = pl.program_id(0)
    qi, ki = qi_ref[step], ki_ref[step]

    @pl.when(ki == 0)
    def _():
        _attn_init(m_ref, l_ref, acc_ref)

    def run(masked):
        mask = None
        if masked:
            kpos = ki * tk + lax.broadcasted_iota(I32, (tk, tq), 0)
            qpos = qi * tq + lax.broadcasted_iota(I32, (tk, tq), 1)
            mask = kpos <= qpos
        kb = kb_ref[...]
        for h in range(nh):
            sl = slice(h * HEAD_DIM, (h + 1) * HEAD_DIM)
            kx = jnp.concatenate([k_ref[:, sl], kb], axis=1)
            qx = jnp.concatenate([q_ref[:, sl], ones_ref[h]], axis=1)
            _attn_head_t(kx, qx, vt_ref[sl, :], mask, h, m_ref, l_ref, acc_ref)

    below_diagonal = ki * tk + tk - 1 <= qi * tq

    @pl.when(below_diagonal)
    def _():
        run(False)

    @pl.when(jnp.logical_not(below_diagonal))
    def _():
        run(True)

    @pl.when(ki == (qi * tq + tq - 1) // tk)
    def _():
        _attn_finish(o_ref, l_ref, acc_ref, nh)


def _fox_prompt(q, k, kb, v_t, tq=512, tk=1024):
    t, w = q.shape
    nh = w // HEAD_DIM
    qi_arr, ki_arr = _causal_steps(t // tq, tq, tk)
    ones = np.zeros((nh, tq, LANES), np.float32)
    for h in range(nh):
        for part in range(3):
            ones[h, :, part * nh + h] = -1.0
    grid_spec = pltpu.PrefetchScalarGridSpec(
        num_scalar_prefetch=2,
        grid=(qi_arr.shape[0],),
        in_specs=[
            pl.BlockSpec((tq, w), lambda s, qi, ki: (qi[s], 0)),
            pl.BlockSpec((tk, w), lambda s, qi, ki: (ki[s], 0)),
            pl.BlockSpec((tk, LANES), lambda s, qi, ki: (ki[s], 0)),
            pl.BlockSpec((w, tk), lambda s, qi, ki: (0, ki[s])),
            pl.BlockSpec((nh, tq, LANES), lambda s, qi, ki: (0, 0, 0)),
        ],
        out_specs=pl.BlockSpec((tq, w), lambda s, qi, ki: (qi[s], 0)),
        scratch_shapes=_attn_scratch(nh, tq),
    )
    return pl.pallas_call(
        functools.partial(_fox_kernel, tq=tq, tk=tk, nh=nh),
        grid_spec=grid_spec,
        out_shape=jax.ShapeDtypeStruct((t, w), BF16),
        compiler_params=_cparams("arbitrary"),
        name="fox_prompt",
    )(qi_arr, ki_arr, q, k, kb, v_t, jnp.asarray(ones, BF16))


def _sortable_key(score):
    bits = lax.bitcast_convert_type(score, I32)
    return jnp.where(bits < 0, bits ^ jnp.int32(0x7FFFFFFF), bits)


def _kth_largest_key(count_ge, shape, topk):
    au = jnp.zeros(shape, I32)
    for bit in range(31, -1, -1):
        bitval = jnp.int32(INT_MIN) if bit == 31 else jnp.int32(1 << bit)
        candu = au | bitval
        cnt = count_ge(candu ^ jnp.int32(INT_MIN))
        au = jnp.where(cnt >= topk, candu, au)
    return jnp.maximum(au ^ jnp.int32(INT_MIN), jnp.int32(INT_MIN + 1))


def _dsa_kernel(qi_ref, ki_ref, qa_ref, ka_ref, vt_ref, qidx_ref, kidx_ref, wit_ref, o_ref,
                keys_ref, thr_ref, qs_ref, m_ref, l_ref, acc_ref, *, tq, tk, tc, nh, topk):
    step = pl.program_id(0)
    qi, ki = qi_ref[step], ki_ref[step]
    last_ki = (qi * tq + tq - 1) // tk

    @pl.when(ki == 0)
    def _index_and_threshold():
        _attn_init(m_ref, l_ref, acc_ref)
        for h in range(H_IDX):
            qs_ref[h * tq:(h + 1) * tq, :] = qidx_ref[:, h * D_IDX:(h + 1) * D_IDX]
        n_causal = ((qi + 1) * tq) // tc
        n_chunks = (last_ki + 1) * (tk // tc)
        qpos = qi * tq + lax.broadcasted_iota(I32, (tc, tq), 1)

        def score_chunk(c, carry):
            off = pl.multiple_of(c * tc, tc)
            s = lax.dot_general(kidx_ref[pl.ds(off, tc), :], qs_ref[...], NT_DIMS, preferred_element_type=F32)
            score = jnp.zeros((tc, tq), F32)
            for h in range(H_IDX):
                score = score + wit_ref[h:h + 1, :] * jnp.maximum(s[:, h * tq:(h + 1) * tq], 0.0)
            kpos = off + lax.broadcasted_iota(I32, (tc, tq), 0)
            keys_ref[pl.ds(off, tc), :] = jnp.where(kpos <= qpos, _sortable_key(score), jnp.int32(INT_MIN))
            return carry

        def mask_chunk(c, carry):
            keys_ref[pl.ds(pl.multiple_of(c * tc, tc), tc), :] = jnp.full((tc, tq), INT_MIN, I32)
            return carry

        lax.fori_loop(0, n_causal, score_chunk, 0)
        lax.fori_loop(n_causal, n_chunks, mask_chunk, 0)

        def count_ge(cand):
            def body(c, acc):
                off = pl.multiple_of(c * tc, tc)
                ge = (keys_ref[pl.ds(off, tc), :] >= cand).astype(I32)
                for b in range(tc // 8):
                    acc = acc + ge[b * 8:(b + 1) * 8, :]
                return acc
            acc = lax.fori_loop(0, n_causal, body, jnp.zeros((8, tq), I32))
            return jnp.sum(acc, axis=0, keepdims=True)

        thr_ref[...] = _kth_largest_key(count_ge, (1, tq), topk)

    off_k = pl.multiple_of(ki * tk, tk)
    selected = keys_ref[pl.ds(off_k, tk), :] >= thr_ref[...]
    for h in range(nh):
        sl = slice(h * HEAD_DIM, (h + 1) * HEAD_DIM)
        _attn_head_t(ka_ref[:, sl], qa_ref[:, sl], vt_ref[sl, :], selected, h, m_ref, l_ref, acc_ref)

    @pl.when(ki == last_ki)
    def _():
        _attn_finish(o_ref, l_ref, acc_ref, nh)


def _dsa_prompt(q_a, k_a, v_a_t, q_i, k_i, w_i_t, tq=512, tk=1024, tc=128):
    t, w = q_a.shape
    nh = w // HEAD_DIM
    topk = min(TOPK_MAX, t // 4)
    qi_arr, ki_arr = _causal_steps(t // tq, tq, tk)
    grid_spec = pltpu.PrefetchScalarGridSpec(
        num_scalar_prefetch=2,
        grid=(qi_arr.shape[0],),
        in_specs=[
            pl.BlockSpec((tq, w), lambda s, qi, ki: (qi[s], 0)),
            pl.BlockSpec((tk, w), lambda s, qi, ki: (ki[s], 0)),
            pl.BlockSpec((w, tk), lambda s, qi, ki: (0, ki[s])),
            pl.BlockSpec((tq, H_IDX * D_IDX), lambda s, qi, ki: (qi[s], 0)),
            pl.BlockSpec((t, D_IDX), lambda s, qi, ki: (0, 0)),
            pl.BlockSpec((H_IDX, tq), lambda s, qi, ki: (0, qi[s])),
        ],
        out_specs=pl.BlockSpec((tq, w), lambda s, qi, ki: (qi[s], 0)),
        scratch_shapes=[
            pltpu.VMEM((t, tq), I32),
            pltpu.VMEM((1, tq), I32),
            pltpu.VMEM((H_IDX * tq, D_IDX), BF16),
        ] + _attn_scratch(nh, tq),
    )
    return pl.pallas_call(
        functools.partial(_dsa_kernel, tq=tq, tk=tk, tc=tc, nh=nh, topk=topk),
        grid_spec=grid_spec,
        out_shape=jax.ShapeDtypeStruct((t, w), BF16),
        compiler_params=_cparams("arbitrary"),
        name="dsa_prompt",
    )(qi_arr, ki_arr, q_a, k_a, v_a_t, q_i, k_i.astype(BF16), w_i_t)


def _sample_score_kernel(pt_ref, q_ref, w_ref, knew_ref, *rest, n_pages, page):
    kpages = rest[:n_pages]
    key_ref, self_ref = rest[n_pages], rest[n_pages + 1]
    q = q_ref[...]
    w = w_ref[...]

    def key_of(kmat):
        s = lax.dot_general(q, kmat, NT_DIMS, preferred_element_type=F32)
        return _sortable_key(jnp.sum(w * jnp.maximum(s, 0.0), axis=0, keepdims=True))

    for j in range(n_pages):
        key_ref[:, j * page:(j + 1) * page] = key_of(kpages[j][...].astype(BF16))
    knew = jnp.broadcast_to(knew_ref[...].astype(BF16), (8, D_IDX))
    self_ref[...] = jnp.broadcast_to(key_of(knew)[:, 0:1], self_ref.shape)


def _sample_scores(q_i, w_i, k_i_new, cache_kidx, page_table):
    n, n_pages = page_table.shape
    page = cache_kidx.shape[1]
    past = n_pages * page
    pt_flat = page_table.reshape(-1)
    kspecs = [pl.BlockSpec((None, page, D_IDX), functools.partial(lambda i, pt, j: (pt[i * n_pages + j], 0, 0), j=j))
              for j in range(n_pages)]
    grid_spec = pltpu.PrefetchScalarGridSpec(
        num_scalar_prefetch=1,
        grid=(n,),
        in_specs=[pl.BlockSpec((None, H_IDX, D_IDX), lambda i, pt: (i, 0, 0)),
                  pl.BlockSpec((None, H_IDX, 1), lambda i, pt: (i, 0, 0)),
                  pl.BlockSpec((None, 1, D_IDX), lambda i, pt: (i, 0, 0))] + kspecs,
        out_specs=[pl.BlockSpec((None, 1, past), lambda i, pt: (i, 0, 0)),
                   pl.BlockSpec((None, 1, LANES), lambda i, pt: (i, 0, 0))],
    )
    keys, self_key = pl.pallas_call(
        functools.partial(_sample_score_kernel, n_pages=n_pages, page=page),
        grid_spec=grid_spec,
        out_shape=[jax.ShapeDtypeStruct((n, 1, past), I32), jax.ShapeDtypeStruct((n, 1, LANES), I32)],
        compiler_params=_cparams("arbitrary"),
        name="sample_scores",
    )(pt_flat, q_i.reshape(n, H_IDX, D_IDX), w_i.reshape(n, H_IDX, 1), k_i_new.reshape(n, 1, D_IDX),
      *([cache_kidx] * n_pages))
    return keys.reshape(n, past), self_key.reshape(n, LANES)


def _sample_select_kernel(key_ref, self_ref, expand_ref, bias_ref, selfb_ref, *, n_pages, page, nh, topk):
    keys = key_ref[...]
    self_key = self_ref[:, 0:1]

    def count_ge(cand):
        past = jnp.sum((keys >= cand).astype(I32), axis=1, keepdims=True)
        return past + (self_key >= cand).astype(I32)

    thr = _kth_largest_key(count_ge, self_key.shape, topk)
    expand = expand_ref[...]
    rows = page * nh
    for j in range(n_pages):
        sel = jnp.where(keys[:, j * page:(j + 1) * page] >= thr, 1.0, 0.0).astype(BF16)
        sel_x = jnp.dot(sel, expand, preferred_element_type=F32)
        bias_ref[:, j * rows:(j + 1) * rows] = (1.0 - sel_x) * NEG
    selfb_ref[...] = jnp.broadcast_to(jnp.where(self_key >= thr, 0.0, NEG), selfb_ref.shape)


def _sample_select(keys, self_key, page, nh):
    n, past = keys.shape
    n_pages = past // page
    topk = min(TOPK_MAX, (past + 1) // 4)
    expand = np.zeros((page, page * nh), np.float32)
    for k in range(page):
        expand[k, k * nh:(k + 1) * nh] = 1.0
    return pl.pallas_call(
        functools.partial(_sample_select_kernel, n_pages=n_pages, page=page, nh=nh, topk=topk),
        out_shape=[jax.ShapeDtypeStruct((n, past * nh), F32), jax.ShapeDtypeStruct((n, LANES), F32)],
        compiler_params=_cparams(),
        name="sample_select",
    )(keys, self_key, jnp.asarray(expand, BF16))


def _sample_attn_kernel(pt_ref, q_ref, knew_ref, vnew_ref, selfb_ref, *rest, pps, nh, fox):
    kp = rest[:pps]
    vp = rest[pps:2 * pps]
    if fox:
        lfp, pre_ref = rest[2 * pps:3 * pps], rest[3 * pps]
        o_ref, m_ref, l_ref, acc_ref, carry_ref = rest[3 * pps + 1:]
    else:
        bias_ref = rest[2 * pps]
        o_ref, m_ref, l_ref, acc_ref = rest[2 * pps + 1:]
    p_step = pl.program_id(1)
    rows = kp[0].shape[0]
    sub = lax.broadcasted_iota(I32, (nh, rows), 0)
    lane = lax.broadcasted_iota(I32, (nh, rows), 1)
    own_head = lax.rem(lane, nh) == sub
    q = q_ref[...]

    @pl.when(p_step == 0)
    def _():
        m_ref[...] = jnp.full(m_ref.shape, NEG, F32)
        l_ref[...] = jnp.zeros(l_ref.shape, F32)
        acc_ref[...] = jnp.zeros(acc_ref.shape, F32)
        if fox:
            carry_ref[...] = jnp.zeros(carry_ref.shape, F32)

    scores = []
    if fox:
        lfs = [lfp[j][...] for j in range(pps)]
        stacked = jnp.concatenate([part for lf in lfs for part in _split3(lf)], axis=0).astype(BF16)
        pre_all = jnp.dot(stacked, pre_ref[...], preferred_element_type=F32)
        carry = carry_ref[...]
    for j in range(pps):
        s = lax.dot_general(q, kp[j][...].astype(BF16), NT_DIMS, preferred_element_type=F32)
        if fox:
            pre = carry
            for part in range(3):
                pre = pre + pre_all[(3 * j + part) * nh:(3 * j + part + 1) * nh]
            carry = carry + jnp.sum(lfs[j], axis=1, keepdims=True)
            s = s - pre
        else:
            s = s + bias_ref[j]
        scores.append(jnp.where(own_head, s, NEG))
    if fox:
        carry_ref[...] = carry

    m_prev = m_ref[...]
    m_new = m_prev
    for s in scores:
        m_new = jnp.maximum(m_new, jnp.max(s, axis=1, keepdims=True))
    alpha = jnp.exp(m_prev - m_new)
    l_new = alpha * l_ref[...]
    acc = alpha * acc_ref[...]
    for j, s in enumerate(scores):
        p = jnp.exp(s - m_new)
        l_new = l_new + jnp.sum(p, axis=1, keepdims=True)
        acc = acc + jnp.dot(p.astype(BF16), vp[j][...].astype(BF16), preferred_element_type=F32)
    m_ref[...] = m_new
    l_ref[...] = l_new
    acc_ref[...] = acc

    @pl.when(p_step == pl.num_programs(1) - 1)
    def _():
        knew = knew_ref[...].astype(BF16).astype(F32)
        s_self = jnp.sum(q.astype(F32) * knew, axis=1, keepdims=True)
        if fox:
            s_self = s_self - (carry_ref[...] + selfb_ref[...])
        else:
            s_self = s_self + selfb_ref[...]
        m_prev = m_ref[...]
        m_new = jnp.maximum(m_prev, s_self)
        alpha = jnp.exp(m_prev - m_new)
        p = jnp.exp(s_self - m_new)
        l_fin = alpha * l_ref[...] + p
        vnew = vnew_ref[...].astype(BF16).astype(F32)
        acc = alpha * acc_ref[...] + p.astype(BF16).astype(F32) * vnew
        o_ref[...] = (acc / l_fin).astype(o_ref.dtype)


def _sample_attn(q, k_new, v_new, cache_k, cache_v, page_table, selfb, extra, fox, pps=4):
    n, nh, hd = q.shape
    n_pages = page_table.shape[1]
    rows = cache_k.shape[1]
    page = rows // nh
    pt_flat = page_table.reshape(-1)

    def page_spec(j, shape):
        return pl.BlockSpec((None,) + shape, lambda i, p, pt: (pt[i * n_pages + p * pps + j], 0, 0))

    per_seq = lambda shape: pl.BlockSpec((None,) + shape, lambda i, p, pt: (i, 0, 0))
    in_specs = [per_seq((nh, hd))] * 3 + [per_seq((nh, 1))]
    in_specs += [page_spec(j, (rows, hd)) for j in range(pps)] * 2
    args = [q, k_new, v_new, selfb] + [cache_k] * pps + [cache_v] * pps
    scratch = [pltpu.VMEM((nh, 1), F32), pltpu.VMEM((nh, 1), F32), pltpu.VMEM((nh, hd), F32)]
    if fox:
        cache_lf_t = extra
        prefix_expand = np.zeros((page, rows), np.float32)
        for k in range(page):
            prefix_expand[k, k * nh:] = 1.0
        in_specs += [page_spec(j, (nh, page)) for j in range(pps)]
        in_specs += [pl.BlockSpec((page, rows), lambda i, p, pt: (0, 0))]
        args += [cache_lf_t] * pps + [jnp.asarray(prefix_expand, BF16)]
        scratch += [pltpu.VMEM((nh, 1), F32)]
    else:
        bias = extra
        in_specs += [pl.BlockSpec((None, pps, 1, rows), lambda i, p, pt: (i, p, 0, 0))]
        args += [bias]
    grid_spec = pltpu.PrefetchScalarGridSpec(
        num_scalar_prefetch=1,
        grid=(n, n_pages // pps),
        in_specs=in_specs,
        out_specs=per_seq((nh, hd)),
        scratch_shapes=scratch,
    )
    return pl.pallas_call(
        functools.partial(_sample_attn_kernel, pps=pps, nh=nh, fox=fox),
        grid_spec=grid_spec,
        out_shape=jax.ShapeDtypeStruct((n, nh, hd), BF16),
        compiler_params=_cparams("arbitrary", "arbitrary"),
        name="sample_fox" if fox else "sample_dsa",
    )(pt_flat, *args)


def _merge_kernel(x_ref, oa_ref, ob_ref, g_ref, wao_ref, wbo_ref, wout_ref, gffn_ref, wr_ref, hn_in_ref,
                  h_ref, hn_ref, eid_ref, gate_ref):
    del hn_in_ref
    d = x_ref.shape[1]
    br_a = jnp.dot(oa_ref[...], wao_ref[...], preferred_element_type=F32)
    br_b = jnp.dot(ob_ref[...], wbo_ref[...], preferred_element_type=F32)
    mix = g_ref[:, :d] * br_a + g_ref[:, d:] * br_b
    h = x_ref[...] + jnp.dot(mix.astype(BF16), wout_ref[...], preferred_element_type=F32)
    h_ref[...] = h
    hn = h * lax.rsqrt(jnp.mean(h * h, axis=-1, keepdims=True) + EPS) * gffn_ref[...]
    hn_ref[...] = hn

    logits = lax.dot_general(wr_ref[...], hn, NT_DIMS, preferred_element_type=F32,
                             precision=lax.Precision.HIGHEST)
    tm = logits.shape[1]
    gl = logits[0:N_GROUPS]
    row_g = lax.broadcasted_iota(I32, gl.shape, 0)
    gmax = jnp.max(gl, axis=0, keepdims=True)
    g_idx = jnp.min(jnp.where(gl == gmax, row_g, N_GROUPS), axis=0, keepdims=True)
    g_w = 1.0 / jnp.sum(jnp.exp(gl - gmax), axis=0, keepdims=True)
    in_l = jnp.zeros((EXPERTS_PER_GROUP, tm), F32)
    for g in range(N_GROUPS):
        lo = 8 + g * EXPERTS_PER_GROUP
        in_l = in_l + jnp.where(g_idx == g, logits[lo:lo + EXPERTS_PER_GROUP], 0.0)
    e = jnp.exp(in_l - jnp.max(in_l, axis=0, keepdims=True))
    in_p = e / jnp.sum(e, axis=0, keepdims=True)
    row_e = lax.broadcasted_iota(I32, in_p.shape, 0)
    p1 = jnp.max(in_p, axis=0, keepdims=True)
    i1 = jnp.min(jnp.where(in_p == p1, row_e, EXPERTS_PER_GROUP), axis=0, keepdims=True)
    rest = jnp.where(row_e == i1, -1.0, in_p)
    p2 = jnp.max(rest, axis=0, keepdims=True)
    i2 = jnp.min(jnp.where(rest == p2, row_e, EXPERTS_PER_GROUP), axis=0, keepdims=True)
    denom = p1 + p2
    row8 = lax.broadcasted_iota(I32, (8, tm), 0)
    eid_ref[...] = jnp.where(row8 == 0, g_idx * EXPERTS_PER_GROUP + i1,
                             jnp.where(row8 == 1, g_idx * EXPERTS_PER_GROUP + i2, 0))
    gate_ref[...] = jnp.where(row8 == 0, p1 / denom * g_w, jnp.where(row8 == 1, p2 / denom * g_w, 0.0))


def _merge(x2d, o_a, o_b, gates, w_a_o, w_b_o, w_out, g_ffn, wr_t, hn_all, row_off, tm=256):
    m, d = x2d.shape
    tm = min(tm, m)
    assert row_off % tm == 0
    blk_off = row_off // tm
    wa = o_a.shape[1]
    const = lambda shape: pl.BlockSpec(shape, lambda i: (0, 0), pipeline_mode=pl.Buffered(1))
    return pl.pallas_call(
        _merge_kernel,
        grid=(m // tm,),
        in_specs=[pl.BlockSpec((tm, d), lambda i: (i, 0)),
                  pl.BlockSpec((tm, wa), lambda i: (i, 0)),
                  pl.BlockSpec((tm, wa), lambda i: (i, 0)),
                  pl.BlockSpec((tm, 2 * d), lambda i: (i, 0)),
                  const((wa, d)), const((wa, d)), const((d, d)), const((1, d)), const(wr_t.shape),
                  pl.BlockSpec(memory_space=pl.ANY)],
        out_specs=[pl.BlockSpec((tm, d), lambda i: (i, 0)), pl.BlockSpec((tm, d), lambda i: (i + blk_off, 0)),
                   pl.BlockSpec((8, tm), lambda i: (0, i)), pl.BlockSpec((8, tm), lambda i: (0, i))],
        out_shape=[jax.ShapeDtypeStruct((m, d), F32), jax.ShapeDtypeStruct(hn_all.shape, F32),
                   jax.ShapeDtypeStruct((8, m), I32), jax.ShapeDtypeStruct((8, m), F32)],
        input_output_aliases={9: 1},
        compiler_params=_cparams("parallel"),
        name="merge_router",
    )(x2d, o_a, o_b, gates, w_a_o, w_b_o, w_out, g_ffn.reshape(1, d), wr_t, hn_all)


def _row_token_kernel(dest_ref, rt_ref, *, n_assign, m_tot, n_rows):
    def init(r, carry):
        rt_ref[r] = 0
        return carry

    def scatter(a, carry):
        rt_ref[dest_ref[a]] = jnp.where(a >= m_tot, a - m_tot, a)
        return carry

    lax.fori_loop(0, n_rows, init, 0, unroll=8)
    lax.fori_loop(0, n_assign, scatter, 0, unroll=8)


def _row_tokens(dest, m_tot, n_rows):
    n_assign = dest.shape[0]
    grid_spec = pltpu.PrefetchScalarGridSpec(
        num_scalar_prefetch=1,
        grid=(1,),
        in_specs=[],
        out_specs=pl.BlockSpec(memory_space=pltpu.SMEM),
    )
    return pl.pallas_call(
        functools.partial(_row_token_kernel, n_assign=n_assign, m_tot=m_tot, n_rows=n_rows),
        grid_spec=grid_spec,
        out_shape=jax.ShapeDtypeStruct((n_rows,), I32),
        compiler_params=_cparams("arbitrary"),
        name="moe_row_tokens",
    )(dest)


def _expert_kernel(te_ref, nv_ref, rt_ref, hn_ref, w1_ref, w3_ref, wd_ref, y_ref, xbuf, w1_s, w3_s, wd_s, sems, *, tm):
    i = pl.program_id(0)
    n_valid = nv_ref[0]

    def row_copy(tile, slot, r):
        tok = rt_ref[tile * tm + r]
        return pltpu.make_async_copy(hn_ref.at[pl.ds(tok, 1)], xbuf.at[slot, pl.ds(r, 1)], sems.at[slot])

    def start_gather(tile, slot):
        def body(r, carry):
            row_copy(tile, slot, r).start()
            return carry
        lax.fori_loop(0, tm, body, 0, unroll=8)

    def wait_gather(tile, slot):
        def body(r, carry):
            row_copy(tile, slot, r).wait()
            return carry
        lax.fori_loop(0, tm, body, 0, unroll=8)

    @pl.when(i == 0)
    def _():
        start_gather(0, 0)

    @pl.when(i + 1 < n_valid)
    def _():
        start_gather(i + 1, (i + 1) % 2)

    changed = jnp.logical_or(i == 0, te_ref[i] != te_ref[jnp.maximum(i - 1, 0)])

    @pl.when(changed)
    def _():
        w1_s[...] = w1_ref[...].astype(BF16)
        w3_s[...] = w3_ref[...].astype(BF16)
        wd_s[...] = wd_ref[...].astype(BF16)

    @pl.when(i < n_valid)
    def _():
        slot = i % 2
        wait_gather(i, slot)
        x = xbuf[slot].astype(BF16)
        u1 = jnp.dot(x, w1_s[...], preferred_element_type=F32)
        u3 = jnp.dot(x, w3_s[...], preferred_element_type=F32)
        hidden = u1 / (1.0 + jnp.exp(-u1)) * u3
        y_ref[...] = jnp.dot(hidden.astype(BF16), wd_s[...], preferred_element_type=F32)

    @pl.when(i >= n_valid)
    def _():
        y_ref[...] = jnp.zeros(y_ref.shape, F32)


def _experts(tile_expert, n_valid, row_token, hn_all, w_up1, w_up3, w_down, tm):
    n_rows = row_token.shape[0]
    d = hn_all.shape[1]
    f = w_up1.shape[2]
    grid_spec = pltpu.PrefetchScalarGridSpec(
        num_scalar_prefetch=3,
        grid=(n_rows // tm,),
        in_specs=[pl.BlockSpec(memory_space=pl.ANY),
                  pl.BlockSpec((None, d, f), lambda i, te, nv, rt: (te[i], 0, 0)),
                  pl.BlockSpec((None, d, f), lambda i, te, nv, rt: (te[i], 0, 0)),
                  pl.BlockSpec((None, f, d), lambda i, te, nv, rt: (te[i], 0, 0))],
        out_specs=pl.BlockSpec((tm, d), lambda i, te, nv, rt: (i, 0)),
        scratch_shapes=[pltpu.VMEM((2, tm, d), F32),
                        pltpu.VMEM((d, f), BF16), pltpu.VMEM((d, f), BF16), pltpu.VMEM((f, d), BF16),
                        pltpu.SemaphoreType.DMA((2,))],
    )
    return pl.pallas_call(
        functools.partial(_expert_kernel, tm=tm),
        grid_spec=grid_spec,
        out_shape=jax.ShapeDtypeStruct((n_rows, d), F32),
        compiler_params=_cparams("arbitrary"),
        name="moe_experts",
    )(tile_expert, n_valid, row_token, hn_all, w_up1, w_up3, w_down)


def _combine_kernel(dest_ref, h_ref, gate_ref, gfin_ref, ys_ref, y_ref, buf0, buf1, sem, *, tok_off, m_tot, tm):
    base = tok_off + pl.program_id(0) * tm

    def copies(r):
        return (pltpu.make_async_copy(ys_ref.at[pl.ds(dest_ref[base + r], 1)], buf0.at[pl.ds(r, 1)], sem),
                pltpu.make_async_copy(ys_ref.at[pl.ds(dest_ref[m_tot + base + r], 1)], buf1.at[pl.ds(r, 1)], sem))

    def start(r, c):
        for cp in copies(r):
            cp.start()
        return c

    def wait(r, c):
        for cp in copies(r):
            cp.wait()
        return c

    lax.fori_loop(0, tm, start, 0)
    lax.fori_loop(0, tm, wait, 0)
    out = h_ref[...] + gate_ref[:, 0:1] * buf0[...] + gate_ref[:, 1:2] * buf1[...]
    y = out * lax.rsqrt(jnp.mean(out * out, axis=-1, keepdims=True) + EPS)
    y_ref[...] = y * gfin_ref[...]


def _combine(dest, h, gate_cols, g_final, ys, tok_off, m_tot, tm=128):
    m, d = h.shape
    tm = min(tm, m)
    grid_spec = pltpu.PrefetchScalarGridSpec(
        num_scalar_prefetch=1,
        grid=(m // tm,),
        in_specs=[pl.BlockSpec((tm, d), lambda i, dr: (i, 0)),
                  pl.BlockSpec((tm, 2), lambda i, dr: (i, 0)),
                  pl.BlockSpec((1, d), lambda i, dr: (0, 0)),
                  pl.BlockSpec(memory_space=pl.ANY)],
        out_specs=pl.BlockSpec((tm, d), lambda i, dr: (i, 0)),
        scratch_shapes=[pltpu.VMEM((tm, d), F32), pltpu.VMEM((tm, d), F32), pltpu.SemaphoreType.DMA(())],
    )
    return pl.pallas_call(
        functools.partial(_combine_kernel, tok_off=tok_off, m_tot=m_tot, tm=tm),
        grid_spec=grid_spec,
        out_shape=jax.ShapeDtypeStruct((m, d), F32),
        compiler_params=_cparams("arbitrary"),
        name="moe_combine",
    )(dest, h, gate_cols, g_final.reshape(1, d), ys)


def _moe_plan(eids, tm):
    e_flat = eids.reshape(-1)
    n_assign = e_flat.shape[0]
    onehot = (e_flat[:, None] == jnp.arange(N_EXPERTS, dtype=I32)[None, :]).astype(I32)
    csum = jnp.cumsum(onehot, axis=0)
    rank = jnp.sum((csum - onehot) * onehot, axis=1)
    counts = csum[-1]
    tiles_e = (counts + tm - 1) // tm
    tile_end = jnp.cumsum(tiles_e)
    dest = ((tile_end - tiles_e) * tm)[e_flat] + rank
    n_tiles = n_assign // tm + N_EXPERTS
    n_valid = tile_end[-1]
    tile_ids = jnp.minimum(jnp.arange(n_tiles, dtype=I32), n_valid - 1)
    tile_expert = jnp.sum((tile_end[None, :] <= tile_ids[:, None]).astype(I32), axis=1)
    return dest.astype(I32), tile_expert.astype(I32), n_valid.reshape(1).astype(I32), n_tiles


def kernel(x_prompt, x_sample, cache_k_a, cache_v_a, cache_kidx_a, cache_k_b, cache_v_b, cache_logf_b,
           page_table, g_mix, w_in, f_bias, w_a_o, w_b_o, w_out, g_ffn, w_grp, w_exp, w_up1, w_up3,
           w_down, g_final):
    depth = w_in.shape[0]
    assert depth == 1, "single-layer stack"
    batch, seq, d = x_prompt.shape
    n_dec, dec_seq, _ = x_sample.shape
    assert batch == 1 and dec_seq == 1
    n_pool, page = cache_k_a.shape[1], cache_k_a.shape[2]
    n_pages = page_table.shape[1]
    past = n_pages * page
    l = 0

    pp = _project_group(x_prompt.reshape(seq, d), jnp.arange(seq), g_mix[l], w_in[l], f_bias[l], True)
    ps = _project_group(x_sample.reshape(n_dec, d), jnp.full((n_dec,), past, I32), g_mix[l], w_in[l], f_bias[l], False)
    w_a = pp["q_a"].shape[1]
    h_a = h_b = w_a // HEAD_DIM

    o_a_p = _dsa_prompt(pp["q_a"], pp["k_a"], pp["v_a_t"], pp["q_i"], pp["k_i"], pp["w_i"].T)
    c_parts = _cumsum_parts(pp["log_f"].T)
    kb = jnp.pad(c_parts.reshape(3 * h_b, seq).T, ((0, 0), (0, LANES - 3 * h_b))).astype(BF16)
    o_b_p = _fox_prompt(pp["q_b"], pp["k_b"], kb, pp["v_b_t"])

    paged = lambda c: c[l].reshape(n_pool, page * h_a, HEAD_DIM)
    per_head = lambda a: a.reshape(n_dec, h_a, HEAD_DIM)
    keys_s, self_key = _sample_scores(ps["q_i"], ps["w_i"], ps["k_i"], cache_kidx_a[l], page_table)
    bias_s, selfb = _sample_select(keys_s, self_key, page, h_a)
    selfb_a = jnp.broadcast_to(selfb[:, 0:1, None], (n_dec, h_a, 1))
    o_a_s = _sample_attn(per_head(ps["q_a"]), per_head(ps["k_a32"]), per_head(ps["v_a32"]),
                         paged(cache_k_a), paged(cache_v_a), page_table, selfb_a,
                         bias_s.reshape(n_dec, n_pages, 1, page * h_a), fox=False).reshape(n_dec, w_a)
    cache_lf_t = jnp.swapaxes(cache_logf_b[l], 1, 2)
    o_b_s = _sample_attn(per_head(ps["q_b"]), per_head(ps["k_b32"]), per_head(ps["v_b32"]),
                         paged(cache_k_b), paged(cache_v_b), page_table, ps["log_f"].reshape(n_dec, h_b, 1),
                         cache_lf_t, fox=True).reshape(n_dec, w_a)

    m_tot = seq + n_dec
    wr_t = jnp.zeros((8 + N_EXPERTS, d), F32).at[:N_GROUPS].set(w_grp[l].T).at[8:].set(w_exp[l].T)
    wao, wbo, wo = w_a_o[l].astype(BF16), w_b_o[l].astype(BF16), w_out[l].astype(BF16)
    hn_all = jnp.zeros((m_tot, d), F32)
    h_s, hn_all, eid_s, gate_s = _merge(x_sample.reshape(n_dec, d), o_a_s, o_b_s, ps["gates"], wao, wbo, wo,
                                        g_ffn[l], wr_t, hn_all, seq)
    h_p, hn_all, eid_p, gate_p = _merge(x_prompt.reshape(seq, d), o_a_p, o_b_p, pp["gates"], wao, wbo, wo,
                                        g_ffn[l], wr_t, hn_all, 0)

    tm_e = 256
    eids = jnp.concatenate([eid_p[:2], eid_s[:2]], axis=1)
    dest, tile_expert, n_valid, n_tiles = _moe_plan(eids, tm_e)
    row_token = _row_tokens(dest, m_tot, n_tiles * tm_e)
    ys = _experts(tile_expert, n_valid, row_token, hn_all, w_up1[l], w_up3[l], w_down[l], tm_e)
    y_p = _combine(dest, h_p, gate_p[:2].T, g_final, ys, 0, m_tot)
    y_s = _combine(dest, h_s, gate_s[:2].T, g_final, ys, seq, m_tot)

    kv = lambda a, b, t, nh: a.reshape(depth, b, t, nh, HEAD_DIM)
    return (y_p.reshape(batch, seq, d), y_s.reshape(n_dec, dec_seq, d),
            kv(pp["k_a32"], batch, seq, h_a), kv(pp["v_a32"], batch, seq, h_a),
            pp["k_i"].reshape(depth, batch, seq, D_IDX),
            kv(pp["k_b32"], batch, seq, h_b), kv(pp["v_b32"], batch, seq, h_b),
            pp["log_f"].reshape(depth, batch, seq, h_b),
            kv(ps["k_a32"], n_dec, dec_seq, h_a), kv(ps["v_a32"], n_dec, dec_seq, h_a),
            ps["k_i"].reshape(depth, n_dec, dec_seq, D_IDX),
            kv(ps["k_b32"], n_dec, dec_seq, h_b), kv(ps["v_b32"], n_dec, dec_seq, h_b),
            ps["log_f"].reshape(depth, n_dec, dec_seq, h_b))
```

```python
import functools

import numpy as np
import jax
import jax.numpy as jnp
from jax import lax
from jax.experimental import pallas as pl
from jax.experimental.pallas import tpu as pltpu

F32 = jnp.float32
BF16 = jnp.bfloat16
I32 = jnp.int32

HEAD_DIM = 128
H_IDX = 16
D_IDX = 64
TOPK_MAX = 256
ROT_A = HEAD_DIM // 4
ROT_IDX = D_IDX // 4
ROPE_THETA = 500000.0
N_GROUPS = 4
EXPERTS_PER_GROUP = 8
N_EXPERTS = N_GROUPS * EXPERTS_PER_GROUP
EPS = 1e-6

LANES = 128
NEG = -1e30
INT_MIN = -(2 ** 31)
VMEM_LIMIT = 56 * 1024 * 1024

NT_DIMS = (((1,), (1,)), ((), ()))


def _cparams(*sem, vmem=VMEM_LIMIT):
    return pltpu.CompilerParams(dimension_semantics=sem, vmem_limit_bytes=vmem)


def _rmsnorm_kernel(x_ref, g_ref, o_ref):
    x = x_ref[...]
    y = x * lax.rsqrt(jnp.mean(x * x, axis=-1, keepdims=True) + EPS)
    o_ref[...] = (y * g_ref[...]).astype(o_ref.dtype)


def _rmsnorm(x, g, out_dtype, tm):
    m, d = x.shape
    return pl.pallas_call(
        _rmsnorm_kernel,
        grid=(m // tm,),
        in_specs=[pl.BlockSpec((tm, d), lambda i: (i, 0)), pl.BlockSpec((1, d), lambda i: (0, 0))],
        out_specs=pl.BlockSpec((tm, d), lambda i: (i, 0)),
        out_shape=jax.ShapeDtypeStruct((m, d), out_dtype),
        compiler_params=_cparams("parallel"),
        name="rmsnorm",
    )(x, g.reshape(1, d))


def _rope_tables(pos, rot_dim, period, active_lanes):
    half = rot_dim // 2
    inv_freq = ROPE_THETA ** (-jnp.arange(half, dtype=F32) / half)
    ang = pos.astype(F32)[:, None] * inv_freq[None, :]
    cos, sin = jnp.cos(ang), jnp.sin(ang)
    lane = np.arange(LANES)
    d = lane % period
    first = (d < half) & (lane < active_lanes)
    second = (d >= half) & (d < rot_dim) & (lane < active_lanes)
    idx = np.where(first, d, np.where(second, d - half, 0))
    cos_l, sin_l = cos[:, idx], sin[:, idx]
    rot = jnp.asarray(first | second)[None, :]
    c = jnp.where(rot, cos_l, 1.0)
    s1 = jnp.where(jnp.asarray(first)[None, :], -sin_l, 0.0)
    s2 = jnp.where(jnp.asarray(second)[None, :], sin_l, 0.0)
    return c, s1, s2


def _rope_lanes(y, c, s1, s2, half):
    return y * c + pltpu.roll(y, LANES - half, 1) * s1 + pltpu.roll(y, half, 1) * s2


def _proj_kernel(*refs, mode, half, n_out, out_scale):
    x_ref, w_ref = refs[0], refs[1]
    outs = refs[len(refs) - n_out:]
    acc = jnp.dot(x_ref[...], w_ref[...], preferred_element_type=F32)
    tn = acc.shape[1]
    if mode == "rope":
        c, s1, s2 = refs[2][...], refs[3][...], refs[4][...]
        for b in range(tn // LANES):
            sl = slice(b * LANES, (b + 1) * LANES)
            y = _rope_lanes(acc[:, sl], c, s1, s2, half)
            if out_scale != 1.0:
                y = y * out_scale
            for o in outs:
                o[:, sl] = y.astype(o.dtype)
        return
    if mode == "sigmoid":
        acc = 1.0 / (1.0 + jnp.exp(-acc))
    if out_scale != 1.0:
        acc = acc * out_scale
    for o in outs:
        o[...] = acc.astype(o.dtype)


def _proj_t_kernel(x_ref, wt_ref, o_ref):
    o_ref[...] = lax.dot_general(wt_ref[...], x_ref[...], NT_DIMS, preferred_element_type=F32).astype(o_ref.dtype)


def _proj_t(xn, w_t, tm=512):
    m, d = xn.shape
    n = w_t.shape[0]
    tm = min(tm, m)
    return pl.pallas_call(
        _proj_t_kernel,
        grid=(m // tm,),
        in_specs=[pl.BlockSpec((tm, d), lambda i: (i, 0)), pl.BlockSpec((n, d), lambda i: (0, 0))],
        out_specs=pl.BlockSpec((n, tm), lambda i: (0, i)),
        out_shape=jax.ShapeDtypeStruct((n, m), BF16),
        compiler_params=_cparams("parallel"),
        name="proj_t",
    )(xn, w_t)


def _proj(xn, w, out_dtypes, mode="none", tables=None, half=0, tm=512, tn=1024, out_scale=1.0):
    m, d = xn.shape
    n = w.shape[1]
    tm = min(tm, m)
    in_specs = [pl.BlockSpec((tm, d), lambda j, i: (i, 0)), pl.BlockSpec((d, tn), lambda j, i: (0, j))]
    args = [xn, w]
    if mode == "rope":
        in_specs += [pl.BlockSpec((tm, LANES), lambda j, i: (i, 0))] * 3
        args += list(tables)
    outs = pl.pallas_call(
        functools.partial(_proj_kernel, mode=mode, half=half, n_out=len(out_dtypes), out_scale=out_scale),
        grid=(n // tn, m // tm),
        in_specs=in_specs,
        out_specs=[pl.BlockSpec((tm, tn), lambda j, i: (i, j)) for _ in out_dtypes],
        out_shape=[jax.ShapeDtypeStruct((m, n), dt) for dt in out_dtypes],
        compiler_params=_cparams("parallel", "parallel"),
        name="proj_" + mode,
    )(*args)
    return outs


def _proj_small_kernel(x_ref, w_ref, c_ref, s1_ref, s2_ref, bias_ref, o_ref, *, half, wi_scale):
    acc = jnp.dot(x_ref[...], w_ref[...], preferred_element_type=F32)
    roped = _rope_lanes(acc, c_ref[...], s1_ref[...], s2_ref[...], half)
    z = acc + bias_ref[...]
    logsig = -(jnp.maximum(-z, 0.0) + jnp.log(1.0 + jnp.exp(-jnp.abs(z))))
    lane = lax.broadcasted_iota(I32, acc.shape, 1)
    o_ref[...] = jnp.where(lane < D_IDX, roped, jnp.where(lane < D_IDX + H_IDX, acc * wi_scale, logsig))


def _proj_small(xn, w_small, tables, bias_row, tm=512):
    m, d = xn.shape
    tm = min(tm, m)
    wi_scale = (H_IDX ** -0.5) * (D_IDX ** -0.5)
    return pl.pallas_call(
        functools.partial(_proj_small_kernel, half=ROT_IDX // 2, wi_scale=wi_scale),
        grid=(m // tm,),
        in_specs=[pl.BlockSpec((tm, d), lambda i: (i, 0)), pl.BlockSpec((d, LANES), lambda i: (0, 0))]
        + [pl.BlockSpec((tm, LANES), lambda i: (i, 0))] * 3 + [pl.BlockSpec((1, LANES), lambda i: (0, 0))],
        out_specs=pl.BlockSpec((tm, LANES), lambda i: (i, 0)),
        out_shape=jax.ShapeDtypeStruct((m, LANES), F32),
        compiler_params=_cparams("parallel"),
        name="proj_small",
    )(xn, w_small, *tables, bias_row)


def _project_group(x2d, pos, g_mix, w_in, f_bias, feature_major_v):
    m, d = x2d.shape
    h_a = h_b = (w_in.shape[1] - (H_IDX * D_IDX + D_IDX + H_IDX) - 2 * d) // (6 * HEAD_DIM + 1)
    w_a = h_a * HEAD_DIM
    sizes = (w_a, w_a, w_a, H_IDX * D_IDX, D_IDX, H_IDX, w_a, w_a, w_a, h_b, d, d)
    offs = np.concatenate([[0], np.cumsum(sizes)])
    seg = lambda k: w_in[:, offs[k]:offs[k + 1]].astype(BF16)

    xn = _rmsnorm(x2d, g_mix, BF16, tm=min(512, m))
    tab_a = _rope_tables(pos, ROT_A, HEAD_DIM, LANES)
    tab_i = _rope_tables(pos, ROT_IDX, D_IDX, LANES)
    tab_k = _rope_tables(pos, ROT_IDX, D_IDX, ROT_IDX)

    qk_scale = HEAD_DIM ** -0.5
    (q_a,) = _proj(xn, seg(0), [BF16], "rope", tab_a, ROT_A // 2, out_scale=qk_scale)
    k_a32, k_a = _proj(xn, seg(1), [F32, BF16], "rope", tab_a, ROT_A // 2)
    (v_a32,) = _proj(xn, seg(2), [F32])
    (q_i,) = _proj(xn, seg(3), [BF16], "rope", tab_i, ROT_IDX // 2)
    (q_b,) = _proj(xn, seg(6), [BF16], out_scale=qk_scale)
    k_b32, k_b = _proj(xn, seg(7), [F32, BF16])
    (v_b32,) = _proj(xn, seg(8), [F32])
    (gates,) = _proj(xn, w_in[:, offs[10]:offs[12]].astype(BF16), [F32], "sigmoid")

    n_small = D_IDX + H_IDX + h_b
    w_small = jnp.concatenate([w_in[:, offs[4]:offs[6]], w_in[:, offs[9]:offs[10]],
                               jnp.zeros((d, LANES - n_small), F32)], axis=1).astype(BF16)
    bias_row = jnp.zeros((1, LANES), F32).at[0, D_IDX + H_IDX:n_small].set(f_bias)
    small = _proj_small(xn, w_small, tab_k, bias_row)
    k_i = small[:, :D_IDX]
    w_i = small[:, D_IDX:D_IDX + H_IDX]
    log_f = small[:, D_IDX + H_IDX:n_small]
    out = dict(q_a=q_a, k_a=k_a, k_a32=k_a32, v_a32=v_a32, q_i=q_i, k_i=k_i, w_i=w_i,
               q_b=q_b, k_b=k_b, k_b32=k_b32, v_b32=v_b32, log_f=log_f, gates=gates)
    if feature_major_v:
        out["v_a_t"] = _proj_t(xn, seg(2).T)
        out["v_b_t"] = _proj_t(xn, seg(8).T)
    return out


def _split3(x):
    hi = x.astype(BF16).astype(F32)
    r1 = x - hi
    mid = r1.astype(BF16).astype(F32)
    lo = (r1 - mid).astype(BF16).astype(F32)
    return hi, mid, lo


def _cumsum_kernel(x_ref, o_ref, *, blk):
    n = x_ref.shape[1]
    r = lax.broadcasted_iota(I32, (blk, blk), 0)
    c = lax.broadcasted_iota(I32, (blk, blk), 1)
    tri = (r <= c).astype(F32)

    def body(i, carry):
        off = pl.multiple_of(i * blk, blk)
        xb = x_ref[:, pl.ds(off, blk)]
        cs = jnp.dot(xb, tri, preferred_element_type=F32, precision=lax.Precision.HIGHEST) + carry
        for j, part in enumerate(_split3(cs)):
            o_ref[j, :, pl.ds(off, blk)] = part
        return cs[:, blk - 1:blk]

    lax.fori_loop(0, n // blk, body, jnp.zeros((x_ref.shape[0], 1), F32))


def _cumsum_parts(x_t, blk=LANES):
    return pl.pallas_call(
        functools.partial(_cumsum_kernel, blk=blk),
        out_shape=jax.ShapeDtypeStruct((3,) + x_t.shape, F32),
        compiler_params=_cparams(),
        name="cumsum_logf",
    )(x_t)


def _causal_steps(nq, tq, tk):
    qi_l, ki_l = [], []
    for qi in range(nq):
        for ki in range((qi * tq + tq - 1) // tk + 1):
            qi_l.append(qi)
            ki_l.append(ki)
    return jnp.asarray(np.array(qi_l, np.int32)), jnp.asarray(np.array(ki_l, np.int32))


def _attn_head_t(kx, qx, vt_h, mask, h, m_ref, l_ref, acc_ref):
    s = lax.dot_general(kx, qx, NT_DIMS, preferred_element_type=F32)
    if mask is not None:
        s = jnp.where(mask, s, NEG)
    m_prev = m_ref[h]
    m_new = jnp.maximum(m_prev, jnp.max(s, axis=0, keepdims=True))
    alpha = jnp.exp(m_prev - m_new)
    p = jnp.exp(s - m_new)
    l_ref[h] = alpha * l_ref[h] + jnp.sum(p, axis=0, keepdims=True)
    acc_ref[h] = alpha * acc_ref[h] + jnp.dot(vt_h, p.astype(BF16), preferred_element_type=F32)
    m_ref[h] = m_new


def _attn_init(m_ref, l_ref, acc_ref):
    m_ref[...] = jnp.full(m_ref.shape, NEG, F32)
    l_ref[...] = jnp.zeros(l_ref.shape, F32)
    acc_ref[...] = jnp.zeros(acc_ref.shape, F32)


def _attn_finish(o_ref, l_ref, acc_ref, nh):
    for h in range(nh):
        o_ref[:, h * HEAD_DIM:(h + 1) * HEAD_DIM] = (acc_ref[h] / l_ref[h]).T.astype(o_ref.dtype)


def _attn_scratch(nh, tq):
    return [pltpu.VMEM((nh, 1, tq), F32), pltpu.VMEM((nh, 1, tq), F32), pltpu.VMEM((nh, HEAD_DIM, tq), F32)]


def _fox_kernel(qi_ref, ki_ref, q_ref, k_ref, kb_ref, vt_ref, ones_ref, o_ref, m_ref, l_ref, acc_ref, *, tq, tk, nh):
    step = pl.program_id(0)
    qi, ki = qi_ref[step], ki_ref[step]

    @pl.when(ki == 0)
    def _():
        _attn_init(m_ref, l_ref, acc_ref)

    def run(masked):
        mask = None
        if masked:
            kpos = ki * tk + lax.broadcasted_iota(I32, (tk, tq), 0)
            qpos = qi * tq + lax.broadcasted_iota(I32, (tk, tq), 1)
            mask = kpos <= qpos
        kb = kb_ref[...]
        for h in range(nh):
            sl = slice(h * HEAD_DIM, (h + 1) * HEAD_DIM)
            kx = jnp.concatenate([k_ref[:, sl], kb], axis=1)
            qx = jnp.concatenate([q_ref[:, sl], ones_ref[h]], axis=1)
            _attn_head_t(kx, qx, vt_ref[sl, :], mask, h, m_ref, l_ref, acc_ref)

    below_diagonal = ki * tk + tk - 1 <= qi * tq

    @pl.when(below_diagonal)
    def _():
        run(False)

    @pl.when(jnp.logical_not(below_diagonal))
    def _():
        run(True)

    @pl.when(ki == (qi * tq + tq - 1) // tk)
    def _():
        _attn_finish(o_ref, l_ref, acc_ref, nh)


def _fox_prompt(q, k, kb, v_t, tq=512, tk=1024):
    t, w = q.shape
    nh = w // HEAD_DIM
    qi_arr, ki_arr = _causal_steps(t // tq, tq, tk)
    ones = np.zeros((nh, tq, LANES), np.float32)
    for h in range(nh):
        for part in range(3):
            ones[h, :, part * nh + h] = -1.0
    grid_spec = pltpu.PrefetchScalarGridSpec(
        num_scalar_prefetch=2,
        grid=(qi_arr.shape[0],),
        in_specs=[
            pl.BlockSpec((tq, w), lambda s, qi, ki: (qi[s], 0)),
            pl.BlockSpec((tk, w), lambda s, qi, ki: (ki[s], 0)),
            pl.BlockSpec((tk, LANES), lambda s, qi, ki: (ki[s], 0)),
            pl.BlockSpec((w, tk), lambda s, qi, ki: (0, ki[s])),
            pl.BlockSpec((nh, tq, LANES), lambda s, qi, ki: (0, 0, 0)),
        ],
        out_specs=pl.BlockSpec((tq, w), lambda s, qi, ki: (qi[s], 0)),
        scratch_shapes=_attn_scratch(nh, tq),
    )
    return pl.pallas_call(
        functools.partial(_fox_kernel, tq=tq, tk=tk, nh=nh),
        grid_spec=grid_spec,
        out_shape=jax.ShapeDtypeStruct((t, w), BF16),
        compiler_params=_cparams("arbitrary"),
        name="fox_prompt",
    )(qi_arr, ki_arr, q, k, kb, v_t, jnp.asarray(ones, BF16))


def _sortable_key(score):
    bits = lax.bitcast_convert_type(score, I32)
    return jnp.where(bits < 0, bits ^ jnp.int32(0x7FFFFFFF), bits)


def _kth_largest_key(count_ge, shape, topk):
    au = jnp.zeros(shape, I32)
    for bit in range(31, -1, -1):
        bitval = jnp.int32(INT_MIN) if bit == 31 else jnp.int32(1 << bit)
        candu = au | bitval
        cnt = count_ge(candu ^ jnp.int32(INT_MIN))
        au = jnp.where(cnt >= topk, candu, au)
    return jnp.maximum(au ^ jnp.int32(INT_MIN), jnp.int32(INT_MIN + 1))


def _dsa_kernel(qi_ref, ki_ref, qa_ref, ka_ref, vt_ref, qidx_ref, kidx_ref, wit_ref, o_ref,
                keys_ref, thr_ref, qs_ref, m_ref, l_ref, acc_ref, *, tq, tk, tc, nh, topk):
    step = pl.program_id(0)
    qi, ki = qi_ref[step], ki_ref[step]
    last_ki = (qi * tq + tq - 1) // tk

    @pl.when(ki == 0)
    def _index_and_threshold():
        _attn_init(m_ref, l_ref, acc_ref)
        for h in range(H_IDX):
            qs_ref[h * tq:(h + 1) * tq, :] = qidx_ref[:, h * D_IDX:(h + 1) * D_IDX]
        n_causal = ((qi + 1) * tq) // tc
        n_chunks = (last_ki + 1) * (tk // tc)
        qpos = qi * tq + lax.broadcasted_iota(I32, (tc, tq), 1)

        def score_chunk(c, carry):
            off = pl.multiple_of(c * tc, tc)
            s = lax.dot_general(kidx_ref[pl.ds(off, tc), :], qs_ref[...], NT_DIMS, preferred_element_type=F32)
            score = jnp.zeros((tc, tq), F32)
            for h in range(H_IDX):
                score = score + wit_ref[h:h + 1, :] * jnp.maximum(s[:, h * tq:(h + 1) * tq], 0.0)
            kpos = off + lax.broadcasted_iota(I32, (tc, tq), 0)
            keys_ref[pl.ds(off, tc), :] = jnp.where(kpos <= qpos, _sortable_key(score), jnp.int32(INT_MIN))
            return carry

        def mask_chunk(c, carry):
            keys_ref[pl.ds(pl.multiple_of(c * tc, tc), tc), :] = jnp.full((tc, tq), INT_MIN, I32)
            return carry

        lax.fori_loop(0, n_causal, score_chunk, 0)
        lax.fori_loop(n_causal, n_chunks, mask_chunk, 0)

        def count_ge(cand):
            def body(g, acc):
                for c in range(tq // tc):
                    off = pl.multiple_of(g * tq + c * tc, tc)
                    ge = (keys_ref[pl.ds(off, tc), :] >= cand).astype(I32)
                    for b in range(tc // 8):
                        acc = acc + ge[b * 8:(b + 1) * 8, :]
                return acc
            acc = lax.fori_loop(0, qi + 1, body, jnp.zeros((8, tq), I32))
            return jnp.sum(acc, axis=0, keepdims=True)

        thr_ref[...] = _kth_largest_key(count_ge, (1, tq), topk)

    off_k = pl.multiple_of(ki * tk, tk)
    selected = keys_ref[pl.ds(off_k, tk), :] >= thr_ref[...]
    for h in range(nh):
        sl = slice(h * HEAD_DIM, (h + 1) * HEAD_DIM)
        _attn_head_t(ka_ref[:, sl], qa_ref[:, sl], vt_ref[sl, :], selected, h, m_ref, l_ref, acc_ref)

    @pl.when(ki == last_ki)
    def _():
        _attn_finish(o_ref, l_ref, acc_ref, nh)


def _dsa_prompt(q_a, k_a, v_a_t, q_i, k_i, w_i_t, tq=512, tk=1024, tc=128):
    t, w = q_a.shape
    nh = w // HEAD_DIM
    topk = min(TOPK_MAX, t // 4)
    qi_arr, ki_arr = _causal_steps(t // tq, tq, tk)
    grid_spec = pltpu.PrefetchScalarGridSpec(
        num_scalar_prefetch=2,
        grid=(qi_arr.shape[0],),
        in_specs=[
            pl.BlockSpec((tq, w), lambda s, qi, ki: (qi[s], 0)),
            pl.BlockSpec((tk, w), lambda s, qi, ki: (ki[s], 0)),
            pl.BlockSpec((w, tk), lambda s, qi, ki: (0, ki[s])),
            pl.BlockSpec((tq, H_IDX * D_IDX), lambda s, qi, ki: (qi[s], 0)),
            pl.BlockSpec((t, D_IDX), lambda s, qi, ki: (0, 0)),
            pl.BlockSpec((H_IDX, tq), lambda s, qi, ki: (0, qi[s])),
        ],
        out_specs=pl.BlockSpec((tq, w), lambda s, qi, ki: (qi[s], 0)),
        scratch_shapes=[
            pltpu.VMEM((t, tq), I32),
            pltpu.VMEM((1, tq), I32),
            pltpu.VMEM((H_IDX * tq, D_IDX), BF16),
        ] + _attn_scratch(nh, tq),
    )
    return pl.pallas_call(
        functools.partial(_dsa_kernel, tq=tq, tk=tk, tc=tc, nh=nh, topk=topk),
        grid_spec=grid_spec,
        out_shape=jax.ShapeDtypeStruct((t, w), BF16),
        compiler_params=_cparams("arbitrary"),
        name="dsa_prompt",
    )(qi_arr, ki_arr, q_a, k_a, v_a_t, q_i, k_i.astype(BF16), w_i_t)


def _sample_score_kernel(pt_ref, q_ref, w_ref, knew_ref, *rest, n_pages, page):
    kpages = rest[:n_pages]
    key_ref, self_ref = rest[n_pages], rest[n_pages + 1]
    q = q_ref[...]
    w = w_ref[...]

    def key_of(s):
        return _sortable_key(jnp.sum(w * jnp.maximum(s, 0.0), axis=0, keepdims=True))

    for j in range(n_pages):
        s = jnp.dot(q, kpages[j][...].astype(BF16), preferred_element_type=F32)
        key_ref[:, j * page:(j + 1) * page] = key_of(s)
    knew = jnp.broadcast_to(knew_ref[...].astype(BF16), (8, D_IDX))
    s_new = lax.dot_general(q, knew, NT_DIMS, preferred_element_type=F32)
    self_ref[...] = jnp.broadcast_to(key_of(s_new)[:, 0:1], self_ref.shape)


def _sample_scores(q_i, w_i, k_i_new, cache_kidx_t, page_table):
    n, n_pages = page_table.shape
    page = cache_kidx_t.shape[2]
    past = n_pages * page
    pt_flat = page_table.reshape(-1)
    kspecs = [pl.BlockSpec((None, D_IDX, page), functools.partial(lambda i, pt, j: (pt[i * n_pages + j], 0, 0), j=j))
              for j in range(n_pages)]
    grid_spec = pltpu.PrefetchScalarGridSpec(
        num_scalar_prefetch=1,
        grid=(n,),
        in_specs=[pl.BlockSpec((None, H_IDX, D_IDX), lambda i, pt: (i, 0, 0)),
                  pl.BlockSpec((None, H_IDX, 1), lambda i, pt: (i, 0, 0)),
                  pl.BlockSpec((None, 1, D_IDX), lambda i, pt: (i, 0, 0))] + kspecs,
        out_specs=[pl.BlockSpec((None, 1, past), lambda i, pt: (i, 0, 0)),
                   pl.BlockSpec((None, 1, LANES), lambda i, pt: (i, 0, 0))],
    )
    keys, self_key = pl.pallas_call(
        functools.partial(_sample_score_kernel, n_pages=n_pages, page=page),
        grid_spec=grid_spec,
        out_shape=[jax.ShapeDtypeStruct((n, 1, past), I32), jax.ShapeDtypeStruct((n, 1, LANES), I32)],
        compiler_params=_cparams("arbitrary"),
        name="sample_scores",
    )(pt_flat, q_i.reshape(n, H_IDX, D_IDX), w_i.reshape(n, H_IDX, 1), k_i_new.reshape(n, 1, D_IDX),
      *([cache_kidx_t] * n_pages))
    return keys.reshape(n, past), self_key.reshape(n, LANES)


def _sample_select_kernel(key_ref, self_ref, expand_ref, bias_ref, selfb_ref, *, n_pages, page, nh, topk):
    keys = key_ref[...]
    self_key = self_ref[:, 0:1]

    def count_ge(cand):
        past = jnp.sum((keys >= cand).astype(I32), axis=1, keepdims=True)
        return past + (self_key >= cand).astype(I32)

    thr = _kth_largest_key(count_ge, self_key.shape, topk)
    expand = expand_ref[...]
    rows = page * nh
    for j in range(n_pages):
        sel = jnp.where(keys[:, j * page:(j + 1) * page] >= thr, 1.0, 0.0).astype(BF16)
        sel_x = jnp.dot(sel, expand, preferred_element_type=F32)
        bias_ref[:, j * rows:(j + 1) * rows] = (1.0 - sel_x) * NEG
    selfb_ref[...] = jnp.broadcast_to(jnp.where(self_key >= thr, 0.0, NEG), selfb_ref.shape)


def _sample_select(keys, self_key, page, nh):
    n, past = keys.shape
    n_pages = past // page
    topk = min(TOPK_MAX, (past + 1) // 4)
    expand = np.zeros((page, page * nh), np.float32)
    for k in range(page):
        expand[k, k * nh:(k + 1) * nh] = 1.0
    return pl.pallas_call(
        functools.partial(_sample_select_kernel, n_pages=n_pages, page=page, nh=nh, topk=topk),
        out_shape=[jax.ShapeDtypeStruct((n, past * nh), F32), jax.ShapeDtypeStruct((n, LANES), F32)],
        compiler_params=_cparams(),
        name="sample_select",
    )(keys, self_key, jnp.asarray(expand, BF16))


def _sample_attn_kernel(pt_ref, q_ref, knew_ref, vnew_ref, selfb_ref, *rest, pps, nh, fox):
    kp = rest[:pps]
    vp = rest[pps:2 * pps]
    if fox:
        lfp, pre_ref = rest[2 * pps:3 * pps], rest[3 * pps]
        o_ref, m_ref, l_ref, acc_ref, carry_ref = rest[3 * pps + 1:]
    else:
        bias_ref = rest[2 * pps]
        o_ref, m_ref, l_ref, acc_ref = rest[2 * pps + 1:]
    p_step = pl.program_id(1)
    rows = kp[0].shape[0]
    sub = lax.broadcasted_iota(I32, (nh, rows), 0)
    lane = lax.broadcasted_iota(I32, (nh, rows), 1)
    own_head = lax.rem(lane, nh) == sub
    q = q_ref[...]

    @pl.when(p_step == 0)
    def _():
        m_ref[...] = jnp.full(m_ref.shape, NEG, F32)
        l_ref[...] = jnp.zeros(l_ref.shape, F32)
        acc_ref[...] = jnp.zeros(acc_ref.shape, F32)
        if fox:
            carry_ref[...] = jnp.zeros(carry_ref.shape, F32)

    scores = []
    if fox:
        lfs = [lfp[j][...] for j in range(pps)]
        stacked = jnp.concatenate([part for lf in lfs for part in _split3(lf)], axis=0).astype(BF16)
        pre_all = jnp.dot(stacked, pre_ref[...], preferred_element_type=F32)
        carry = carry_ref[...]
    for j in range(pps):
        s = lax.dot_general(q, kp[j][...].astype(BF16), NT_DIMS, preferred_element_type=F32)
        if fox:
            pre = carry
            for part in range(3):
                pre = pre + pre_all[(3 * j + part) * nh:(3 * j + part + 1) * nh]
            carry = carry + jnp.sum(lfs[j], axis=1, keepdims=True)
            s = s - pre
        else:
            s = s + bias_ref[j]
        scores.append(jnp.where(own_head, s, NEG))
    if fox:
        carry_ref[...] = carry

    m_prev = m_ref[...]
    m_new = m_prev
    for s in scores:
        m_new = jnp.maximum(m_new, jnp.max(s, axis=1, keepdims=True))
    alpha = jnp.exp(m_prev - m_new)
    l_new = alpha * l_ref[...]
    acc = alpha * acc_ref[...]
    for j, s in enumerate(scores):
        p = jnp.exp(s - m_new)
        l_new = l_new + jnp.sum(p, axis=1, keepdims=True)
        acc = acc + jnp.dot(p.astype(BF16), vp[j][...].astype(BF16), preferred_element_type=F32)
    m_ref[...] = m_new
    l_ref[...] = l_new
    acc_ref[...] = acc

    @pl.when(p_step == pl.num_programs(1) - 1)
    def _():
        knew = knew_ref[...].astype(BF16).astype(F32)
        s_self = jnp.sum(q.astype(F32) * knew, axis=1, keepdims=True)
        if fox:
            s_self = s_self - (carry_ref[...] + selfb_ref[...])
        else:
            s_self = s_self + selfb_ref[...]
        m_prev = m_ref[...]
        m_new = jnp.maximum(m_prev, s_self)
        alpha = jnp.exp(m_prev - m_new)
        p = jnp.exp(s_self - m_new)
        l_fin = alpha * l_ref[...] + p
        vnew = vnew_ref[...].astype(BF16).astype(F32)
        acc = alpha * acc_ref[...] + p.astype(BF16).astype(F32) * vnew
        o_ref[...] = (acc / l_fin).astype(o_ref.dtype)


def _sample_attn(q, k_new, v_new, cache_k, cache_v, page_table, selfb, extra, fox, pps=8):
    n, nh, hd = q.shape
    n_pages = page_table.shape[1]
    rows = cache_k.shape[1]
    page = rows // nh
    pt_flat = page_table.reshape(-1)

    pps = min(pps, n_pages)

    def page_spec(j, shape):
        return pl.BlockSpec((None,) + shape, lambda i, p, pt: (pt[i * n_pages + p * pps + j], 0, 0))

    per_seq = lambda shape: pl.BlockSpec((None,) + shape, lambda i, p, pt: (i, 0, 0))
    in_specs = [per_seq((nh, hd))] * 3 + [per_seq((nh, 1))]
    in_specs += [page_spec(j, (rows, hd)) for j in range(pps)] * 2
    args = [q, k_new, v_new, selfb] + [cache_k] * pps + [cache_v] * pps
    scratch = [pltpu.VMEM((nh, 1), F32), pltpu.VMEM((nh, 1), F32), pltpu.VMEM((nh, hd), F32)]
    if fox:
        cache_lf_t = extra
        prefix_expand = np.zeros((page, rows), np.float32)
        for k in range(page):
            prefix_expand[k, k * nh:] = 1.0
        in_specs += [page_spec(j, (nh, page)) for j in range(pps)]
        in_specs += [pl.BlockSpec((page, rows), lambda i, p, pt: (0, 0))]
        args += [cache_lf_t] * pps + [jnp.asarray(prefix_expand, BF16)]
        scratch += [pltpu.VMEM((nh, 1), F32)]
    else:
        bias = extra
        in_specs += [pl.BlockSpec((None, pps, 1, rows), lambda i, p, pt: (i, p, 0, 0))]
        args += [bias]
    grid_spec = pltpu.PrefetchScalarGridSpec(
        num_scalar_prefetch=1,
        grid=(n, n_pages // pps),
        in_specs=in_specs,
        out_specs=per_seq((nh, hd)),
        scratch_shapes=scratch,
    )
    return pl.pallas_call(
        functools.partial(_sample_attn_kernel, pps=pps, nh=nh, fox=fox),
        grid_spec=grid_spec,
        out_shape=jax.ShapeDtypeStruct((n, nh, hd), BF16),
        compiler_params=_cparams("arbitrary", "arbitrary"),
        name="sample_fox" if fox else "sample_dsa",
    )(pt_flat, *args)


def _merge_kernel(x_ref, oa_ref, ob_ref, g_ref, wao_ref, wbo_ref, wout_ref, gffn_ref, wr_ref, hn_in_ref,
                  h_ref, hn_ref, eid_ref, gate_ref):
    del hn_in_ref
    d = x_ref.shape[1]
    br_a = jnp.dot(oa_ref[...], wao_ref[...], preferred_element_type=F32)
    br_b = jnp.dot(ob_ref[...], wbo_ref[...], preferred_element_type=F32)
    mix = g_ref[:, :d] * br_a + g_ref[:, d:] * br_b
    h = x_ref[...] + jnp.dot(mix.astype(BF16), wout_ref[...], preferred_element_type=F32)
    h_ref[...] = h
    hn = h * lax.rsqrt(jnp.mean(h * h, axis=-1, keepdims=True) + EPS) * gffn_ref[...]
    hn_ref[...] = hn

    logits = lax.dot_general(wr_ref[...], hn, NT_DIMS, preferred_element_type=F32,
                             precision=lax.Precision.HIGHEST)
    tm = logits.shape[1]
    gl = logits[0:N_GROUPS]
    row_g = lax.broadcasted_iota(I32, gl.shape, 0)
    gmax = jnp.max(gl, axis=0, keepdims=True)
    g_idx = jnp.min(jnp.where(gl == gmax, row_g, N_GROUPS), axis=0, keepdims=True)
    g_w = 1.0 / jnp.sum(jnp.exp(gl - gmax), axis=0, keepdims=True)
    in_l = jnp.zeros((EXPERTS_PER_GROUP, tm), F32)
    for g in range(N_GROUPS):
        lo = 8 + g * EXPERTS_PER_GROUP
        in_l = in_l + jnp.where(g_idx == g, logits[lo:lo + EXPERTS_PER_GROUP], 0.0)
    e = jnp.exp(in_l - jnp.max(in_l, axis=0, keepdims=True))
    in_p = e / jnp.sum(e, axis=0, keepdims=True)
    row_e = lax.broadcasted_iota(I32, in_p.shape, 0)
    p1 = jnp.max(in_p, axis=0, keepdims=True)
    i1 = jnp.min(jnp.where(in_p == p1, row_e, EXPERTS_PER_GROUP), axis=0, keepdims=True)
    rest = jnp.where(row_e == i1, -1.0, in_p)
    p2 = jnp.max(rest, axis=0, keepdims=True)
    i2 = jnp.min(jnp.where(rest == p2, row_e, EXPERTS_PER_GROUP), axis=0, keepdims=True)
    denom = p1 + p2
    row8 = lax.broadcasted_iota(I32, (8, tm), 0)
    eid_ref[...] = jnp.where(row8 == 0, g_idx * EXPERTS_PER_GROUP + i1,
                             jnp.where(row8 == 1, g_idx * EXPERTS_PER_GROUP + i2, 0))
    gate_ref[...] = jnp.where(row8 == 0, p1 / denom * g_w, jnp.where(row8 == 1, p2 / denom * g_w, 0.0))


def _merge(x2d, o_a, o_b, gates, w_a_o, w_b_o, w_out, g_ffn, wr_t, hn_all, row_off, tm=256):
    m, d = x2d.shape
    tm = min(tm, m)
    assert row_off % tm == 0
    blk_off = row_off // tm
    wa = o_a.shape[1]
    const = lambda shape: pl.BlockSpec(shape, lambda i: (0, 0), pipeline_mode=pl.Buffered(1))
    return pl.pallas_call(
        _merge_kernel,
        grid=(m // tm,),
        in_specs=[pl.BlockSpec((tm, d), lambda i: (i, 0)),
                  pl.BlockSpec((tm, wa), lambda i: (i, 0)),
                  pl.BlockSpec((tm, wa), lambda i: (i, 0)),
                  pl.BlockSpec((tm, 2 * d), lambda i: (i, 0)),
                  const((wa, d)), const((wa, d)), const((d, d)), const((1, d)), const(wr_t.shape),
                  pl.BlockSpec(memory_space=pl.ANY)],
        out_specs=[pl.BlockSpec((tm, d), lambda i: (i, 0)), pl.BlockSpec((tm, d), lambda i: (i + blk_off, 0)),
                   pl.BlockSpec((8, tm), lambda i: (0, i)), pl.BlockSpec((8, tm), lambda i: (0, i))],
        out_shape=[jax.ShapeDtypeStruct((m, d), F32), jax.ShapeDtypeStruct(hn_all.shape, F32),
                   jax.ShapeDtypeStruct((8, m), I32), jax.ShapeDtypeStruct((8, m), F32)],
        input_output_aliases={9: 1},
        compiler_params=_cparams("parallel"),
        name="merge_router",
    )(x2d, o_a, o_b, gates, w_a_o, w_b_o, w_out, g_ffn.reshape(1, d), wr_t, hn_all)


def _row_token_kernel(dest_ref, rt_ref, *, n_assign, m_tot, n_rows):
    def init(r, carry):
        rt_ref[r] = 0
        return carry

    def scatter(a, carry):
        rt_ref[dest_ref[a]] = jnp.where(a >= m_tot, a - m_tot, a)
        return carry

    lax.fori_loop(0, n_rows, init, 0, unroll=8)
    lax.fori_loop(0, n_assign, scatter, 0, unroll=8)


def _row_tokens(dest, m_tot, n_rows):
    n_assign = dest.shape[0]
    grid_spec = pltpu.PrefetchScalarGridSpec(
        num_scalar_prefetch=1,
        grid=(1,),
        in_specs=[],
        out_specs=pl.BlockSpec(memory_space=pltpu.SMEM),
    )
    return pl.pallas_call(
        functools.partial(_row_token_kernel, n_assign=n_assign, m_tot=m_tot, n_rows=n_rows),
        grid_spec=grid_spec,
        out_shape=jax.ShapeDtypeStruct((n_rows,), I32),
        compiler_params=_cparams("arbitrary"),
        name="moe_row_tokens",
    )(dest)


def _expert_kernel(te_ref, nv_ref, rt_ref, hn_ref, w1_ref, w3_ref, wd_ref, y_ref, xbuf, w1_s, w3_s, wd_s, sems, *, tm):
    i = pl.program_id(0)
    n_valid = nv_ref[0]

    def row_copy(tile, slot, r):
        tok = rt_ref[tile * tm + r]
        return pltpu.make_async_copy(hn_ref.at[pl.ds(tok, 1)], xbuf.at[slot, pl.ds(r, 1)], sems.at[slot])

    def start_gather(tile, slot):
        def body(r, carry):
            row_copy(tile, slot, r).start()
            return carry
        lax.fori_loop(0, tm, body, 0, unroll=8)

    def wait_gather(tile, slot):
        def body(r, carry):
            row_copy(tile, slot, r).wait()
            return carry
        lax.fori_loop(0, tm, body, 0, unroll=8)

    @pl.when(i == 0)
    def _():
        start_gather(0, 0)

    @pl.when(i + 1 < n_valid)
    def _():
        start_gather(i + 1, (i + 1) % 2)

    changed = jnp.logical_or(i == 0, te_ref[i] != te_ref[jnp.maximum(i - 1, 0)])

    @pl.when(changed)
    def _():
        w1_s[...] = w1_ref[...].astype(BF16)
        w3_s[...] = w3_ref[...].astype(BF16)
        wd_s[...] = wd_ref[...].astype(BF16)

    @pl.when(i < n_valid)
    def _():
        slot = i % 2
        wait_gather(i, slot)
        x = xbuf[slot].astype(BF16)
        u1 = jnp.dot(x, w1_s[...], preferred_element_type=F32)
        u3 = jnp.dot(x, w3_s[...], preferred_element_type=F32)
        hidden = u1 / (1.0 + jnp.exp(-u1)) * u3
        y_ref[...] = jnp.dot(hidden.astype(BF16), wd_s[...], preferred_element_type=F32)

    @pl.when(i >= n_valid)
    def _():
        y_ref[...] = jnp.zeros(y_ref.shape, F32)


def _experts(tile_expert, n_valid, row_token, hn_all, w_up1, w_up3, w_down, tm):
    n_rows = row_token.shape[0]
    d = hn_all.shape[1]
    f = w_up1.shape[2]
    grid_spec = pltpu.PrefetchScalarGridSpec(
        num_scalar_prefetch=3,
        grid=(n_rows // tm,),
        in_specs=[pl.BlockSpec(memory_space=pl.ANY),
                  pl.BlockSpec((None, d, f), lambda i, te, nv, rt: (te[i], 0, 0)),
                  pl.BlockSpec((None, d, f), lambda i, te, nv, rt: (te[i], 0, 0)),
                  pl.BlockSpec((None, f, d), lambda i, te, nv, rt: (te[i], 0, 0))],
        out_specs=pl.BlockSpec((tm, d), lambda i, te, nv, rt: (i, 0)),
        scratch_shapes=[pltpu.VMEM((2, tm, d), F32),
                        pltpu.VMEM((d, f), BF16), pltpu.VMEM((d, f), BF16), pltpu.VMEM((f, d), BF16),
                        pltpu.SemaphoreType.DMA((2,))],
    )
    return pl.pallas_call(
        functools.partial(_expert_kernel, tm=tm),
        grid_spec=grid_spec,
        out_shape=jax.ShapeDtypeStruct((n_rows, d), F32),
        compiler_params=_cparams("arbitrary"),
        name="moe_experts",
    )(tile_expert, n_valid, row_token, hn_all, w_up1, w_up3, w_down)


def _combine_kernel(dest_ref, h_ref, gate_ref, gfin_ref, ys_ref, y_ref, buf0, buf1, sem, *, tok_off, m_tot, tm):
    base = tok_off + pl.program_id(0) * tm

    def copies(r):
        return (pltpu.make_async_copy(ys_ref.at[pl.ds(dest_ref[base + r], 1)], buf0.at[pl.ds(r, 1)], sem),
                pltpu.make_async_copy(ys_ref.at[pl.ds(dest_ref[m_tot + base + r], 1)], buf1.at[pl.ds(r, 1)], sem))

    def start(r, c):
        for cp in copies(r):
            cp.start()
        return c

    def wait(r, c):
        for cp in copies(r):
            cp.wait()
        return c

    lax.fori_loop(0, tm, start, 0)
    lax.fori_loop(0, tm, wait, 0)
    out = h_ref[...] + gate_ref[:, 0:1] * buf0[...] + gate_ref[:, 1:2] * buf1[...]
    y = out * lax.rsqrt(jnp.mean(out * out, axis=-1, keepdims=True) + EPS)
    y_ref[...] = y * gfin_ref[...]


def _combine(dest, h, gate_cols, g_final, ys, tok_off, m_tot, tm=128):
    m, d = h.shape
    tm = min(tm, m)
    grid_spec = pltpu.PrefetchScalarGridSpec(
        num_scalar_prefetch=1,
        grid=(m // tm,),
        in_specs=[pl.BlockSpec((tm, d), lambda i, dr: (i, 0)),
                  pl.BlockSpec((tm, 2), lambda i, dr: (i, 0)),
                  pl.BlockSpec((1, d), lambda i, dr: (0, 0)),
                  pl.BlockSpec(memory_space=pl.ANY)],
        out_specs=pl.BlockSpec((tm, d), lambda i, dr: (i, 0)),
        scratch_shapes=[pltpu.VMEM((tm, d), F32), pltpu.VMEM((tm, d), F32), pltpu.SemaphoreType.DMA(())],
    )
    return pl.pallas_call(
        functools.partial(_combine_kernel, tok_off=tok_off, m_tot=m_tot, tm=tm),
        grid_spec=grid_spec,
        out_shape=jax.ShapeDtypeStruct((m, d), F32),
        compiler_params=_cparams("arbitrary"),
        name="moe_combine",
    )(dest, h, gate_cols, g_final.reshape(1, d), ys)


def _moe_plan(eids, tm):
    e_flat = eids.reshape(-1)
    n_assign = e_flat.shape[0]
    onehot = (e_flat[:, None] == jnp.arange(N_EXPERTS, dtype=I32)[None, :]).astype(I32)
    csum = jnp.cumsum(onehot, axis=0)
    rank = jnp.sum((csum - onehot) * onehot, axis=1)
    counts = csum[-1]
    tiles_e = (counts + tm - 1) // tm
    tile_end = jnp.cumsum(tiles_e)
    dest = ((tile_end - tiles_e) * tm)[e_flat] + rank
    n_tiles = n_assign // tm + N_EXPERTS
    n_valid = tile_end[-1]
    tile_ids = jnp.minimum(jnp.arange(n_tiles, dtype=I32), n_valid - 1)
    tile_expert = jnp.sum((tile_end[None, :] <= tile_ids[:, None]).astype(I32), axis=1)
    return dest.astype(I32), tile_expert.astype(I32), n_valid.reshape(1).astype(I32), n_tiles


def kernel(x_prompt, x_sample, cache_k_a, cache_v_a, cache_kidx_a, cache_k_b, cache_v_b, cache_logf_b,
           page_table, g_mix, w_in, f_bias, w_a_o, w_b_o, w_out, g_ffn, w_grp, w_exp, w_up1, w_up3,
           w_down, g_final):
    depth = w_in.shape[0]
    assert depth == 1, "single-layer stack"
    batch, seq, d = x_prompt.shape
    n_dec, dec_seq, _ = x_sample.shape
    assert batch == 1 and dec_seq == 1
    n_pool, page = cache_k_a.shape[1], cache_k_a.shape[2]
    n_pages = page_table.shape[1]
    past = n_pages * page
    l = 0

    pp = _project_group(x_prompt.reshape(seq, d), jnp.arange(seq), g_mix[l], w_in[l], f_bias[l], True)
    ps = _project_group(x_sample.reshape(n_dec, d), jnp.full((n_dec,), past, I32), g_mix[l], w_in[l], f_bias[l], False)
    w_a = pp["q_a"].shape[1]
    h_a = h_b = w_a // HEAD_DIM

    o_a_p = _dsa_prompt(pp["q_a"], pp["k_a"], pp["v_a_t"], pp["q_i"], pp["k_i"], pp["w_i"].T)
    c_parts = _cumsum_parts(pp["log_f"].T)
    kb = jnp.pad(c_parts.reshape(3 * h_b, seq).T, ((0, 0), (0, LANES - 3 * h_b))).astype(BF16)
    o_b_p = _fox_prompt(pp["q_b"], pp["k_b"], kb, pp["v_b_t"])

    paged = lambda c: c[l].reshape(n_pool, page * h_a, HEAD_DIM)
    per_head = lambda a: a.reshape(n_dec, h_a, HEAD_DIM)
    keys_s, self_key = _sample_scores(ps["q_i"], ps["w_i"], ps["k_i"], jnp.swapaxes(cache_kidx_a[l], 1, 2), page_table)
    bias_s, selfb = _sample_select(keys_s, self_key, page, h_a)
    selfb_a = jnp.broadcast_to(selfb[:, 0:1, None], (n_dec, h_a, 1))
    o_a_s = _sample_attn(per_head(ps["q_a"]), per_head(ps["k_a32"]), per_head(ps["v_a32"]),
                         paged(cache_k_a), paged(cache_v_a), page_table, selfb_a,
                         bias_s.reshape(n_dec, n_pages, 1, page * h_a), fox=False).reshape(n_dec, w_a)
    cache_lf_t = jnp.swapaxes(cache_logf_b[l], 1, 2)
    o_b_s = _sample_attn(per_head(ps["q_b"]), per_head(ps["k_b32"]), per_head(ps["v_b32"]),
                         paged(cache_k_b), paged(cache_v_b), page_table, ps["log_f"].reshape(n_dec, h_b, 1),
                         cache_lf_t, fox=True).reshape(n_dec, w_a)

    m_tot = seq + n_dec
    wr_t = jnp.zeros((8 + N_EXPERTS, d), F32).at[:N_GROUPS].set(w_grp[l].T).at[8:].set(w_exp[l].T)
    wao, wbo, wo = w_a_o[l].astype(BF16), w_b_o[l].astype(BF16), w_out[l].astype(BF16)
    hn_all = jnp.zeros((m_tot, d), F32)
    h_s, hn_all, eid_s, gate_s = _merge(x_sample.reshape(n_dec, d), o_a_s, o_b_s, ps["gates"], wao, wbo, wo,
                                        g_ffn[l], wr_t, hn_all, seq)
    h_p, hn_all, eid_p, gate_p = _merge(x_prompt.reshape(seq, d), o_a_p, o_b_p, pp["gates"], wao, wbo, wo,
                                        g_ffn[l], wr_t, hn_all, 0)

    tm_e = 256
    eids = jnp.concatenate([eid_p[:2], eid_s[:2]], axis=1)
    dest, tile_expert, n_valid, n_tiles = _moe_plan(eids, tm_e)
    row_token = _row_tokens(dest, m_tot, n_tiles * tm_e)
    ys = _experts(tile_expert, n_valid, row_token, hn_all, w_up1[l], w_up3[l], w_down[l], tm_e)
    y_p = _combine(dest, h_p, gate_p[:2].T, g_final, ys, 0, m_tot)
    y_s = _combine(dest, h_s, gate_s[:2].T, g_final, ys, seq, m_tot)

    kv = lambda a, b, t, nh: a.reshape(depth, b, t, nh, HEAD_DIM)
    return (y_p.reshape(batch, seq, d), y_s.reshape(n_dec, dec_seq, d),
            kv(pp["k_a32"], batch, seq, h_a), kv(pp["v_a32"], batch, seq, h_a),
            pp["k_i"].reshape(depth, batch, seq, D_IDX),
            kv(pp["k_b32"], batch, seq, h_b), kv(pp["v_b32"], batch, seq, h_b),
            pp["log_f"].reshape(depth, batch, seq, h_b),
            kv(ps["k_a32"], n_dec, dec_seq, h_a), kv(ps["v_a32"], n_dec, dec_seq, h_a),
            ps["k_i"].reshape(depth, n_dec, dec_seq, D_IDX),
            kv(ps["k_b32"], n_dec, dec_seq, h_b), kv(ps["v_b32"], n_dec, dec_seq, h_b),
            ps["log_f"].reshape(depth, n_dec, dec_seq, h_b))
```

```python
import functools

import numpy as np
import jax
import jax.numpy as jnp
from jax import lax
from jax.experimental import pallas as pl
from jax.experimental.pallas import tpu as pltpu

F32 = jnp.float32
BF16 = jnp.bfloat16
I32 = jnp.int32

HEAD_DIM = 128
H_IDX = 16
D_IDX = 64
TOPK_MAX = 256
ROT_A = HEAD_DIM // 4
ROT_IDX = D_IDX // 4
ROPE_THETA = 500000.0
N_GROUPS = 4
EXPERTS_PER_GROUP = 8
N_EXPERTS = N_GROUPS * EXPERTS_PER_GROUP
EPS = 1e-6

LANES = 128
NEG = -1e30
INT_MIN = -(2 ** 31)
VMEM_LIMIT = 56 * 1024 * 1024

NT_DIMS = (((1,), (1,)), ((), ()))


def _cparams(*sem, vmem=VMEM_LIMIT):
    return pltpu.CompilerParams(dimension_semantics=sem, vmem_limit_bytes=vmem)


def _rmsnorm_kernel(x_ref, g_ref, o_ref):
    x = x_ref[...]
    y = x * lax.rsqrt(jnp.mean(x * x, axis=-1, keepdims=True) + EPS)
    o_ref[...] = (y * g_ref[...]).astype(o_ref.dtype)


def _rmsnorm(x, g, out_dtype, tm):
    m, d = x.shape
    return pl.pallas_call(
        _rmsnorm_kernel,
        grid=(m // tm,),
        in_specs=[pl.BlockSpec((tm, d), lambda i: (i, 0)), pl.BlockSpec((1, d), lambda i: (0, 0))],
        out_specs=pl.BlockSpec((tm, d), lambda i: (i, 0)),
        out_shape=jax.ShapeDtypeStruct((m, d), out_dtype),
        compiler_params=_cparams("parallel"),
        name="rmsnorm",
    )(x, g.reshape(1, d))


def _rope_tables(pos, rot_dim, period, active_lanes):
    half = rot_dim // 2
    inv_freq = ROPE_THETA ** (-jnp.arange(half, dtype=F32) / half)
    ang = pos.astype(F32)[:, None] * inv_freq[None, :]
    cos, sin = jnp.cos(ang), jnp.sin(ang)
    lane = np.arange(LANES)
    d = lane % period
    first = (d < half) & (lane < active_lanes)
    second = (d >= half) & (d < rot_dim) & (lane < active_lanes)
    idx = np.where(first, d, np.where(second, d - half, 0))
    cos_l, sin_l = cos[:, idx], sin[:, idx]
    rot = jnp.asarray(first | second)[None, :]
    c = jnp.where(rot, cos_l, 1.0)
    s1 = jnp.where(jnp.asarray(first)[None, :], -sin_l, 0.0)
    s2 = jnp.where(jnp.asarray(second)[None, :], sin_l, 0.0)
    return c, s1, s2


def _rope_lanes(y, c, s1, s2, half):
    return y * c + pltpu.roll(y, LANES - half, 1) * s1 + pltpu.roll(y, half, 1) * s2


def _proj_kernel(*refs, mode, half, n_out, out_scale):
    x_ref, w_ref = refs[0], refs[1]
    outs = refs[len(refs) - n_out:]
    acc = jnp.dot(x_ref[...], w_ref[...], preferred_element_type=F32)
    tn = acc.shape[1]
    if mode == "rope":
        c, s1, s2 = refs[2][...], refs[3][...], refs[4][...]
        for b in range(tn // LANES):
            sl = slice(b * LANES, (b + 1) * LANES)
            y = _rope_lanes(acc[:, sl], c, s1, s2, half)
            if out_scale != 1.0:
                y = y * out_scale
            for o in outs:
                o[:, sl] = y.astype(o.dtype)
        return
    if mode == "sigmoid":
        acc = 1.0 / (1.0 + jnp.exp(-acc))
    if out_scale != 1.0:
        acc = acc * out_scale
    for o in outs:
        o[...] = acc.astype(o.dtype)


def _proj_t_kernel(x_ref, wt_ref, o_ref):
    o_ref[...] = lax.dot_general(wt_ref[...], x_ref[...], NT_DIMS, preferred_element_type=F32).astype(o_ref.dtype)


def _proj_t(xn, w_t, tm=512):
    m, d = xn.shape
    n = w_t.shape[0]
    tm = min(tm, m)
    return pl.pallas_call(
        _proj_t_kernel,
        grid=(m // tm,),
        in_specs=[pl.BlockSpec((tm, d), lambda i: (i, 0)), pl.BlockSpec((n, d), lambda i: (0, 0))],
        out_specs=pl.BlockSpec((n, tm), lambda i: (0, i)),
        out_shape=jax.ShapeDtypeStruct((n, m), BF16),
        compiler_params=_cparams("parallel"),
        name="proj_t",
    )(xn, w_t)


def _proj(xn, w, out_dtypes, mode="none", tables=None, half=0, tm=512, tn=1024, out_scale=1.0):
    m, d = xn.shape
    n = w.shape[1]
    tm = min(tm, m)
    in_specs = [pl.BlockSpec((tm, d), lambda j, i: (i, 0)), pl.BlockSpec((d, tn), lambda j, i: (0, j))]
    args = [xn, w]
    if mode == "rope":
        in_specs += [pl.BlockSpec((tm, LANES), lambda j, i: (i, 0))] * 3
        args += list(tables)
    outs = pl.pallas_call(
        functools.partial(_proj_kernel, mode=mode, half=half, n_out=len(out_dtypes), out_scale=out_scale),
        grid=(n // tn, m // tm),
        in_specs=in_specs,
        out_specs=[pl.BlockSpec((tm, tn), lambda j, i: (i, j)) for _ in out_dtypes],
        out_shape=[jax.ShapeDtypeStruct((m, n), dt) for dt in out_dtypes],
        compiler_params=_cparams("parallel", "parallel"),
        name="proj_" + mode,
    )(*args)
    return outs


def _proj_small_kernel(x_ref, w_ref, c_ref, s1_ref, s2_ref, bias_ref, o_ref, *, half, wi_scale):
    acc = jnp.dot(x_ref[...], w_ref[...], preferred_element_type=F32)
    roped = _rope_lanes(acc, c_ref[...], s1_ref[...], s2_ref[...], half)
    z = acc + bias_ref[...]
    logsig = -(jnp.maximum(-z, 0.0) + jnp.log(1.0 + jnp.exp(-jnp.abs(z))))
    lane = lax.broadcasted_iota(I32, acc.shape, 1)
    o_ref[...] = jnp.where(lane < D_IDX, roped, jnp.where(lane < D_IDX + H_IDX, acc * wi_scale, logsig))


def _proj_small(xn, w_small, tables, bias_row, tm=512):
    m, d = xn.shape
    tm = min(tm, m)
    wi_scale = (H_IDX ** -0.5) * (D_IDX ** -0.5)
    return pl.pallas_call(
        functools.partial(_proj_small_kernel, half=ROT_IDX // 2, wi_scale=wi_scale),
        grid=(m // tm,),
        in_specs=[pl.BlockSpec((tm, d), lambda i: (i, 0)), pl.BlockSpec((d, LANES), lambda i: (0, 0))]
        + [pl.BlockSpec((tm, LANES), lambda i: (i, 0))] * 3 + [pl.BlockSpec((1, LANES), lambda i: (0, 0))],
        out_specs=pl.BlockSpec((tm, LANES), lambda i: (i, 0)),
        out_shape=jax.ShapeDtypeStruct((m, LANES), F32),
        compiler_params=_cparams("parallel"),
        name="proj_small",
    )(xn, w_small, *tables, bias_row)


def _project_group(x2d, pos, g_mix, w_in, f_bias, feature_major_v):
    m, d = x2d.shape
    h_a = h_b = (w_in.shape[1] - (H_IDX * D_IDX + D_IDX + H_IDX) - 2 * d) // (6 * HEAD_DIM + 1)
    w_a = h_a * HEAD_DIM
    sizes = (w_a, w_a, w_a, H_IDX * D_IDX, D_IDX, H_IDX, w_a, w_a, w_a, h_b, d, d)
    offs = np.concatenate([[0], np.cumsum(sizes)])
    seg = lambda k: w_in[:, offs[k]:offs[k + 1]].astype(BF16)

    xn = _rmsnorm(x2d, g_mix, BF16, tm=min(512, m))
    tab_a = _rope_tables(pos, ROT_A, HEAD_DIM, LANES)
    tab_i = _rope_tables(pos, ROT_IDX, D_IDX, LANES)
    tab_k = _rope_tables(pos, ROT_IDX, D_IDX, ROT_IDX)

    qk_scale = HEAD_DIM ** -0.5
    (q_a,) = _proj(xn, seg(0), [BF16], "rope", tab_a, ROT_A // 2, out_scale=qk_scale)
    k_a32, k_a = _proj(xn, seg(1), [F32, BF16], "rope", tab_a, ROT_A // 2)
    (v_a32,) = _proj(xn, seg(2), [F32])
    (q_i,) = _proj(xn, seg(3), [BF16], "rope", tab_i, ROT_IDX // 2)
    (q_b,) = _proj(xn, seg(6), [BF16], out_scale=qk_scale)
    k_b32, k_b = _proj(xn, seg(7), [F32, BF16])
    (v_b32,) = _proj(xn, seg(8), [F32])
    (gates,) = _proj(xn, w_in[:, offs[10]:offs[12]].astype(BF16), [F32], "sigmoid")

    n_small = D_IDX + H_IDX + h_b
    w_small = jnp.concatenate([w_in[:, offs[4]:offs[6]], w_in[:, offs[9]:offs[10]],
                               jnp.zeros((d, LANES - n_small), F32)], axis=1).astype(BF16)
    bias_row = jnp.zeros((1, LANES), F32).at[0, D_IDX + H_IDX:n_small].set(f_bias)
    small = _proj_small(xn, w_small, tab_k, bias_row)
    k_i = small[:, :D_IDX]
    w_i = small[:, D_IDX:D_IDX + H_IDX]
    log_f = small[:, D_IDX + H_IDX:n_small]
    out = dict(q_a=q_a, k_a=k_a, k_a32=k_a32, v_a32=v_a32, q_i=q_i, k_i=k_i, w_i=w_i,
               q_b=q_b, k_b=k_b, k_b32=k_b32, v_b32=v_b32, log_f=log_f, gates=gates)
    if feature_major_v:
        out["v_a_t"] = _proj_t(xn, seg(2).T)
        out["v_b_t"] = _proj_t(xn, seg(8).T)
    return out


def _split3(x):
    hi = x.astype(BF16).astype(F32)
    r1 = x - hi
    mid = r1.astype(BF16).astype(F32)
    lo = (r1 - mid).astype(BF16).astype(F32)
    return hi, mid, lo


def _cumsum_kernel(x_ref, o_ref, *, blk):
    n = x_ref.shape[1]
    r = lax.broadcasted_iota(I32, (blk, blk), 0)
    c = lax.broadcasted_iota(I32, (blk, blk), 1)
    tri = (r <= c).astype(F32)

    def body(i, carry):
        off = pl.multiple_of(i * blk, blk)
        xb = x_ref[:, pl.ds(off, blk)]
        cs = jnp.dot(xb, tri, preferred_element_type=F32, precision=lax.Precision.HIGHEST) + carry
        for j, part in enumerate(_split3(cs)):
            o_ref[j, :, pl.ds(off, blk)] = part
        return cs[:, blk - 1:blk]

    lax.fori_loop(0, n // blk, body, jnp.zeros((x_ref.shape[0], 1), F32))


def _cumsum_parts(x_t, blk=LANES):
    return pl.pallas_call(
        functools.partial(_cumsum_kernel, blk=blk),
        out_shape=jax.ShapeDtypeStruct((3,) + x_t.shape, F32),
        compiler_params=_cparams(),
        name="cumsum_logf",
    )(x_t)


def _causal_steps(nq, tq, tk):
    qi_l, ki_l = [], []
    for qi in range(nq):
        for ki in range((qi * tq + tq - 1) // tk + 1):
            qi_l.append(qi)
            ki_l.append(ki)
    return jnp.asarray(np.array(qi_l, np.int32)), jnp.asarray(np.array(ki_l, np.int32))


def _attn_head_t(kx, qx, vt_h, mask, h, m_ref, l_ref, acc_ref):
    s = lax.dot_general(kx, qx, NT_DIMS, preferred_element_type=F32)
    if mask is not None:
        s = jnp.where(mask, s, NEG)
    m_prev = m_ref[h]
    m_new = jnp.maximum(m_prev, jnp.max(s, axis=0, keepdims=True))
    alpha = jnp.exp(m_prev - m_new)
    p = jnp.exp(s - m_new)
    l_ref[h] = alpha * l_ref[h] + jnp.sum(p, axis=0, keepdims=True)
    acc_ref[h] = alpha * acc_ref[h] + jnp.dot(vt_h, p.astype(BF16), preferred_element_type=F32)
    m_ref[h] = m_new


def _attn_init(m_ref, l_ref, acc_ref):
    m_ref[...] = jnp.full(m_ref.shape, NEG, F32)
    l_ref[...] = jnp.zeros(l_ref.shape, F32)
    acc_ref[...] = jnp.zeros(acc_ref.shape, F32)


def _attn_finish(o_ref, l_ref, acc_ref, nh):
    for h in range(nh):
        o_ref[:, h * HEAD_DIM:(h + 1) * HEAD_DIM] = (acc_ref[h] / l_ref[h]).T.astype(o_ref.dtype)


def _attn_scratch(nh, tq):
    return [pltpu.VMEM((nh, 1, tq), F32), pltpu.VMEM((nh, 1, tq), F32), pltpu.VMEM((nh, HEAD_DIM, tq), F32)]


def _fox_kernel(qi_ref, ki_ref, q_ref, k_ref, kb_ref, vt_ref, ones_ref, o_ref, m_ref, l_ref, acc_ref, *, tq, tk, nh):
    step = pl.program_id(0)
    qi, ki = qi_ref[step], ki_ref[step]

    @pl.when(ki == 0)
    def _():
        _attn_init(m_ref, l_ref, acc_ref)

    def run(masked):
        mask = None
        if masked:
            kpos = ki * tk + lax.broadcasted_iota(I32, (tk, tq), 0)
            qpos = qi * tq + lax.broadcasted_iota(I32, (tk, tq), 1)
            mask = kpos <= qpos
        kb = kb_ref[...]
        for h in range(nh):
            sl = slice(h * HEAD_DIM, (h + 1) * HEAD_DIM)
            kx = jnp.concatenate([k_ref[:, sl], kb], axis=1)
            qx = jnp.concatenate([q_ref[:, sl], ones_ref[h]], axis=1)
            _attn_head_t(kx, qx, vt_ref[sl, :], mask, h, m_ref, l_ref, acc_ref)

    below_diagonal = ki * tk + tk - 1 <= qi * tq

    @pl.when(below_diagonal)
    def _():
        run(False)

    @pl.when(jnp.logical_not(below_diagonal))
    def _():
        run(True)

    @pl.when(ki == (qi * tq + tq - 1) // tk)
    def _():
        _attn_finish(o_ref, l_ref, acc_ref, nh)


def _fox_prompt(q, k, kb, v_t, tq=512, tk=1024):
    t, w = q.shape
    nh = w // HEAD_DIM
    qi_arr, ki_arr = _causal_steps(t // tq, tq, tk)
    ones = np.zeros((nh, tq, LANES), np.float32)
    for h in range(nh):
        for part in range(3):
            ones[h, :, part * nh + h] = -1.0
    grid_spec = pltpu.PrefetchScalarGridSpec(
        num_scalar_prefetch=2,
        grid=(qi_arr.shape[0],),
        in_specs=[
            pl.BlockSpec((tq, w), lambda s, qi, ki: (qi[s], 0)),
            pl.BlockSpec((tk, w), lambda s, qi, ki: (ki[s], 0)),
            pl.BlockSpec((tk, LANES), lambda s, qi, ki: (ki[s], 0)),
            pl.BlockSpec((w, tk), lambda s, qi, ki: (0, ki[s])),
            pl.BlockSpec((nh, tq, LANES), lambda s, qi, ki: (0, 0, 0)),
        ],
        out_specs=pl.BlockSpec((tq, w), lambda s, qi, ki: (qi[s], 0)),
        scratch_shapes=_attn_scratch(nh, tq),
    )
    return pl.pallas_call(
        functools.partial(_fox_kernel, tq=tq, tk=tk, nh=nh),
        grid_spec=grid_spec,
        out_shape=jax.ShapeDtypeStruct((t, w), BF16),
        compiler_params=_cparams("arbitrary"),
        name="fox_prompt",
    )(qi_arr, ki_arr, q, k, kb, v_t, jnp.asarray(ones, BF16))


def _sortable_key(score):
    bits = lax.bitcast_convert_type(score, I32)
    return jnp.where(bits < 0, bits ^ jnp.int32(0x7FFFFFFF), bits)


def _kth_largest_key(count_ge, shape, topk):
    au = jnp.zeros(shape, I32)
    for bit in range(31, -1, -1):
        bitval = jnp.int32(INT_MIN) if bit == 31 else jnp.int32(1 << bit)
        candu = au | bitval
        cnt = count_ge(candu ^ jnp.int32(INT_MIN))
        au = jnp.where(cnt >= topk, candu, au)
    return jnp.maximum(au ^ jnp.int32(INT_MIN), jnp.int32(INT_MIN + 1))


def _dsa_kernel(qi_ref, ki_ref, qa_ref, ka_ref, vt_ref, qidx_ref, kidx_ref, wit_ref, o_ref,
                keys_ref, thr_ref, qs_ref, m_ref, l_ref, acc_ref, *, tq, tk, tc, nh, topk):
    step = pl.program_id(0)
    qi, ki = qi_ref[step], ki_ref[step]
    last_ki = (qi * tq + tq - 1) // tk

    @pl.when(ki == 0)
    def _index_and_threshold():
        _attn_init(m_ref, l_ref, acc_ref)
        for h in range(H_IDX):
            qs_ref[h * tq:(h + 1) * tq, :] = qidx_ref[:, h * D_IDX:(h + 1) * D_IDX]
        n_causal = ((qi + 1) * tq) // tc
        n_chunks = (last_ki + 1) * (tk // tc)
        qpos = qi * tq + lax.broadcasted_iota(I32, (tc, tq), 1)

        def score_chunk(c, carry):
            off = pl.multiple_of(c * tc, tc)
            s = lax.dot_general(kidx_ref[pl.ds(off, tc), :], qs_ref[...], NT_DIMS, preferred_element_type=F32)
            score = jnp.zeros((tc, tq), F32)
            for h in range(H_IDX):
                score = score + wit_ref[h:h + 1, :] * jnp.maximum(s[:, h * tq:(h + 1) * tq], 0.0)
            kpos = off + lax.broadcasted_iota(I32, (tc, tq), 0)
            keys_ref[pl.ds(off, tc), :] = jnp.where(kpos <= qpos, _sortable_key(score), jnp.int32(INT_MIN))
            return carry

        def mask_chunk(c, carry):
            keys_ref[pl.ds(pl.multiple_of(c * tc, tc), tc), :] = jnp.full((tc, tq), INT_MIN, I32)
            return carry

        lax.fori_loop(0, n_causal, score_chunk, 0)
        lax.fori_loop(n_causal, n_chunks, mask_chunk, 0)

        def count_ge(cand):
            def body(g, acc):
                for c in range(tq // tc):
                    off = pl.multiple_of(g * tq + c * tc, tc)
                    ge = (keys_ref[pl.ds(off, tc), :] >= cand).astype(I32)
                    for b in range(tc // 8):
                        acc = acc + ge[b * 8:(b + 1) * 8, :]
                return acc
            acc = lax.fori_loop(0, qi + 1, body, jnp.zeros((8, tq), I32))
            return jnp.sum(acc, axis=0, keepdims=True)

        thr_ref[...] = _kth_largest_key(count_ge, (1, tq), topk)

    off_k = pl.multiple_of(ki * tk, tk)
    selected = keys_ref[pl.ds(off_k, tk), :] >= thr_ref[...]
    for h in range(nh):
        sl = slice(h * HEAD_DIM, (h + 1) * HEAD_DIM)
        _attn_head_t(ka_ref[:, sl], qa_ref[:, sl], vt_ref[sl, :], selected, h, m_ref, l_ref, acc_ref)

    @pl.when(ki == last_ki)
    def _():
        _attn_finish(o_ref, l_ref, acc_ref, nh)


def _dsa_prompt(q_a, k_a, v_a_t, q_i, k_i, w_i_t, tq=512, tk=1024, tc=128):
    t, w = q_a.shape
    nh = w // HEAD_DIM
    topk = min(TOPK_MAX, t // 4)
    qi_arr, ki_arr = _causal_steps(t // tq, tq, tk)
    grid_spec = pltpu.PrefetchScalarGridSpec(
        num_scalar_prefetch=2,
        grid=(qi_arr.shape[0],),
        in_specs=[
            pl.BlockSpec((tq, w), lambda s, qi, ki: (qi[s], 0)),
            pl.BlockSpec((tk, w), lambda s, qi, ki: (ki[s], 0)),
            pl.BlockSpec((w, tk), lambda s, qi, ki: (0, ki[s])),
            pl.BlockSpec((tq, H_IDX * D_IDX), lambda s, qi, ki: (qi[s], 0)),
            pl.BlockSpec((t, D_IDX), lambda s, qi, ki: (0, 0)),
            pl.BlockSpec((H_IDX, tq), lambda s, qi, ki: (0, qi[s])),
        ],
        out_specs=pl.BlockSpec((tq, w), lambda s, qi, ki: (qi[s], 0)),
        scratch_shapes=[
            pltpu.VMEM((t, tq), I32),
            pltpu.VMEM((1, tq), I32),
            pltpu.VMEM((H_IDX * tq, D_IDX), BF16),
        ] + _attn_scratch(nh, tq),
    )
    return pl.pallas_call(
        functools.partial(_dsa_kernel, tq=tq, tk=tk, tc=tc, nh=nh, topk=topk),
        grid_spec=grid_spec,
        out_shape=jax.ShapeDtypeStruct((t, w), BF16),
        compiler_params=_cparams("arbitrary"),
        name="dsa_prompt",
    )(qi_arr, ki_arr, q_a, k_a, v_a_t, q_i, k_i.astype(BF16), w_i_t)


def _sample_score_kernel(pt_ref, q_ref, w_ref, knew_ref, *rest, n_pages, page):
    kpages = rest[:n_pages]
    key_ref, self_ref = rest[n_pages], rest[n_pages + 1]
    q = q_ref[...]
    w = w_ref[...]

    def key_of(s):
        return _sortable_key(jnp.sum(w * jnp.maximum(s, 0.0), axis=0, keepdims=True))

    for j in range(n_pages):
        s = jnp.dot(q, kpages[j][...].astype(BF16), preferred_element_type=F32)
        key_ref[:, j * page:(j + 1) * page] = key_of(s)
    knew = jnp.broadcast_to(knew_ref[...].astype(BF16), (8, D_IDX))
    s_new = lax.dot_general(q, knew, NT_DIMS, preferred_element_type=F32)
    self_ref[...] = jnp.broadcast_to(key_of(s_new)[:, 0:1], self_ref.shape)


def _sample_scores(q_i, w_i, k_i_new, cache_kidx_t, page_table):
    n, n_pages = page_table.shape
    page = cache_kidx_t.shape[2]
    past = n_pages * page
    pt_flat = page_table.reshape(-1)
    kspecs = [pl.BlockSpec((None, D_IDX, page), functools.partial(lambda i, pt, j: (pt[i * n_pages + j], 0, 0), j=j))
              for j in range(n_pages)]
    grid_spec = pltpu.PrefetchScalarGridSpec(
        num_scalar_prefetch=1,
        grid=(n,),
        in_specs=[pl.BlockSpec((None, H_IDX, D_IDX), lambda i, pt: (i, 0, 0)),
                  pl.BlockSpec((None, H_IDX, 1), lambda i, pt: (i, 0, 0)),
                  pl.BlockSpec((None, 1, D_IDX), lambda i, pt: (i, 0, 0))] + kspecs,
        out_specs=[pl.BlockSpec((None, 1, past), lambda i, pt: (i, 0, 0)),
                   pl.BlockSpec((None, 1, LANES), lambda i, pt: (i, 0, 0))],
    )
    keys, self_key = pl.pallas_call(
        functools.partial(_sample_score_kernel, n_pages=n_pages, page=page),
        grid_spec=grid_spec,
        out_shape=[jax.ShapeDtypeStruct((n, 1, past), I32), jax.ShapeDtypeStruct((n, 1, LANES), I32)],
        compiler_params=_cparams("arbitrary"),
        name="sample_scores",
    )(pt_flat, q_i.reshape(n, H_IDX, D_IDX), w_i.reshape(n, H_IDX, 1), k_i_new.reshape(n, 1, D_IDX),
      *([cache_kidx_t] * n_pages))
    return keys.reshape(n, past), self_key.reshape(n, LANES)


def _sample_select_kernel(key_ref, self_ref, expand_ref, bias_ref, selfb_ref, *, n_pages, page, nh, topk):
    keys = key_ref[...]
    self_key = self_ref[:, 0:1]

    def count_ge(cand):
        past = jnp.sum((keys >= cand).astype(I32), axis=1, keepdims=True)
        return past + (self_key >= cand).astype(I32)

    thr = _kth_largest_key(count_ge, self_key.shape, topk)
    expand = expand_ref[...]
    rows = page * nh
    for j in range(n_pages):
        sel = jnp.where(keys[:, j * page:(j + 1) * page] >= thr, 1.0, 0.0).astype(BF16)
        sel_x = jnp.dot(sel, expand, preferred_element_type=F32)
        bias_ref[:, j * rows:(j + 1) * rows] = (1.0 - sel_x) * NEG
    selfb_ref[...] = jnp.broadcast_to(jnp.where(self_key >= thr, 0.0, NEG), selfb_ref.shape)


def _sample_select(keys, self_key, page, nh):
    n, past = keys.shape
    n_pages = past // page
    topk = min(TOPK_MAX, (past + 1) // 4)
    expand = np.zeros((page, page * nh), np.float32)
    for k in range(page):
        expand[k, k * nh:(k + 1) * nh] = 1.0
    return pl.pallas_call(
        functools.partial(_sample_select_kernel, n_pages=n_pages, page=page, nh=nh, topk=topk),
        out_shape=[jax.ShapeDtypeStruct((n, past * nh), F32), jax.ShapeDtypeStruct((n, LANES), F32)],
        compiler_params=_cparams(),
        name="sample_select",
    )(keys, self_key, jnp.asarray(expand, BF16))


def _sample_attn_kernel(pt_ref, q_ref, knew_ref, vnew_ref, selfb_ref, *rest, pps, nh, fox):
    kp = rest[:pps]
    vp = rest[pps:2 * pps]
    if fox:
        lfp, pre_ref = rest[2 * pps:3 * pps], rest[3 * pps]
        o_ref, m_ref, l_ref, acc_ref, carry_ref = rest[3 * pps + 1:]
    else:
        bias_ref = rest[2 * pps]
        o_ref, m_ref, l_ref, acc_ref = rest[2 * pps + 1:]
    p_step = pl.program_id(1)
    rows = kp[0].shape[0]
    sub = lax.broadcasted_iota(I32, (nh, rows), 0)
    lane = lax.broadcasted_iota(I32, (nh, rows), 1)
    own_head = lax.rem(lane, nh) == sub
    q = q_ref[...]

    @pl.when(p_step == 0)
    def _():
        m_ref[...] = jnp.full(m_ref.shape, NEG, F32)
        l_ref[...] = jnp.zeros(l_ref.shape, F32)
        acc_ref[...] = jnp.zeros(acc_ref.shape, F32)
        if fox:
            carry_ref[...] = jnp.zeros(carry_ref.shape, F32)

    scores = []
    if fox:
        lfs = [lfp[j][...] for j in range(pps)]
        stacked = jnp.concatenate([part for lf in lfs for part in _split3(lf)], axis=0).astype(BF16)
        pre_all = jnp.dot(stacked, pre_ref[...], preferred_element_type=F32)
        carry = carry_ref[...]
    for j in range(pps):
        s = lax.dot_general(q, kp[j][...].astype(BF16), NT_DIMS, preferred_element_type=F32)
        if fox:
            pre = carry
            for part in range(3):
                pre = pre + pre_all[(3 * j + part) * nh:(3 * j + part + 1) * nh]
            carry = carry + jnp.sum(lfs[j], axis=1, keepdims=True)
            s = s - pre
        else:
            s = s + bias_ref[j]
        scores.append(jnp.where(own_head, s, NEG))
    if fox:
        carry_ref[...] = carry

    m_prev = m_ref[...]
    m_new = m_prev
    for s in scores:
        m_new = jnp.maximum(m_new, jnp.max(s, axis=1, keepdims=True))
    alpha = jnp.exp(m_prev - m_new)
    l_new = alpha * l_ref[...]
    acc = alpha * acc_ref[...]
    for j, s in enumerate(scores):
        p = jnp.exp(s - m_new)
        l_new = l_new + jnp.sum(p, axis=1, keepdims=True)
        acc = acc + jnp.dot(p.astype(BF16), vp[j][...].astype(BF16), preferred_element_type=F32)
    m_ref[...] = m_new
    l_ref[...] = l_new
    acc_ref[...] = acc

    @pl.when(p_step == pl.num_programs(1) - 1)
    def _():
        knew = knew_ref[...].astype(BF16).astype(F32)
        s_self = jnp.sum(q.astype(F32) * knew, axis=1, keepdims=True)
        if fox:
            s_self = s_self - (carry_ref[...] + selfb_ref[...])
        else:
            s_self = s_self + selfb_ref[...]
        m_prev = m_ref[...]
        m_new = jnp.maximum(m_prev, s_self)
        alpha = jnp.exp(m_prev - m_new)
        p = jnp.exp(s_self - m_new)
        l_fin = alpha * l_ref[...] + p
        vnew = vnew_ref[...].astype(BF16).astype(F32)
        acc = alpha * acc_ref[...] + p.astype(BF16).astype(F32) * vnew
        o_ref[...] = (acc / l_fin).astype(o_ref.dtype)


def _sample_attn(q, k_new, v_new, cache_k, cache_v, page_table, selfb, extra, fox, pps=8):
    n, nh, hd = q.shape
    n_pages = page_table.shape[1]
    rows = cache_k.shape[1]
    page = rows // nh
    pt_flat = page_table.reshape(-1)

    pps = min(pps, n_pages)

    def page_spec(j, shape):
        return pl.BlockSpec((None,) + shape, lambda i, p, pt: (pt[i * n_pages + p * pps + j], 0, 0))

    per_seq = lambda shape: pl.BlockSpec((None,) + shape, lambda i, p, pt: (i, 0, 0))
    in_specs = [per_seq((nh, hd))] * 3 + [per_seq((nh, 1))]
    in_specs += [page_spec(j, (rows, hd)) for j in range(pps)] * 2
    args = [q, k_new, v_new, selfb] + [cache_k] * pps + [cache_v] * pps
    scratch = [pltpu.VMEM((nh, 1), F32), pltpu.VMEM((nh, 1), F32), pltpu.VMEM((nh, hd), F32)]
    if fox:
        cache_lf_t = extra
        prefix_expand = np.zeros((page, rows), np.float32)
        for k in range(page):
            prefix_expand[k, k * nh:] = 1.0
        in_specs += [page_spec(j, (nh, page)) for j in range(pps)]
        in_specs += [pl.BlockSpec((page, rows), lambda i, p, pt: (0, 0))]
        args += [cache_lf_t] * pps + [jnp.asarray(prefix_expand, BF16)]
        scratch += [pltpu.VMEM((nh, 1), F32)]
    else:
        bias = extra
        in_specs += [pl.BlockSpec((None, pps, 1, rows), lambda i, p, pt: (i, p, 0, 0))]
        args += [bias]
    grid_spec = pltpu.PrefetchScalarGridSpec(
        num_scalar_prefetch=1,
        grid=(n, n_pages // pps),
        in_specs=in_specs,
        out_specs=per_seq((nh, hd)),
        scratch_shapes=scratch,
    )
    return pl.pallas_call(
        functools.partial(_sample_attn_kernel, pps=pps, nh=nh, fox=fox),
        grid_spec=grid_spec,
        out_shape=jax.ShapeDtypeStruct((n, nh, hd), BF16),
        compiler_params=_cparams("arbitrary", "arbitrary"),
        name="sample_fox" if fox else "sample_dsa",
    )(pt_flat, *args)


def _merge_kernel(x_ref, oa_ref, ob_ref, g_ref, wao_ref, wbo_ref, wout_ref, gffn_ref, wr_ref, hn_in_ref,
                  h_ref, hn_ref, eid_ref, gate_ref):
    del hn_in_ref
    d = x_ref.shape[1]
    br_a = jnp.dot(oa_ref[...], wao_ref[...], preferred_element_type=F32)
    br_b = jnp.dot(ob_ref[...], wbo_ref[...], preferred_element_type=F32)
    mix = g_ref[:, :d] * br_a + g_ref[:, d:] * br_b
    h = x_ref[...] + jnp.dot(mix.astype(BF16), wout_ref[...], preferred_element_type=F32)
    h_ref[...] = h
    hn = h * lax.rsqrt(jnp.mean(h * h, axis=-1, keepdims=True) + EPS) * gffn_ref[...]
    hn_ref[...] = hn

    logits = lax.dot_general(wr_ref[...], hn, NT_DIMS, preferred_element_type=F32,
                             precision=lax.Precision.HIGHEST)
    tm = logits.shape[1]
    gl = logits[0:N_GROUPS]
    row_g = lax.broadcasted_iota(I32, gl.shape, 0)
    gmax = jnp.max(gl, axis=0, keepdims=True)
    g_idx = jnp.min(jnp.where(gl == gmax, row_g, N_GROUPS), axis=0, keepdims=True)
    g_w = 1.0 / jnp.sum(jnp.exp(gl - gmax), axis=0, keepdims=True)
    in_l = jnp.zeros((EXPERTS_PER_GROUP, tm), F32)
    for g in range(N_GROUPS):
        lo = 8 + g * EXPERTS_PER_GROUP
        in_l = in_l + jnp.where(g_idx == g, logits[lo:lo + EXPERTS_PER_GROUP], 0.0)
    e = jnp.exp(in_l - jnp.max(in_l, axis=0, keepdims=True))
    in_p = e / jnp.sum(e, axis=0, keepdims=True)
    row_e = lax.broadcasted_iota(I32, in_p.shape, 0)
    p1 = jnp.max(in_p, axis=0, keepdims=True)
    i1 = jnp.min(jnp.where(in_p == p1, row_e, EXPERTS_PER_GROUP), axis=0, keepdims=True)
    rest = jnp.where(row_e == i1, -1.0, in_p)
    p2 = jnp.max(rest, axis=0, keepdims=True)
    i2 = jnp.min(jnp.where(rest == p2, row_e, EXPERTS_PER_GROUP), axis=0, keepdims=True)
    denom = p1 + p2
    row8 = lax.broadcasted_iota(I32, (8, tm), 0)
    eid_ref[...] = jnp.where(row8 == 0, g_idx * EXPERTS_PER_GROUP + i1,
                             jnp.where(row8 == 1, g_idx * EXPERTS_PER_GROUP + i2, 0))
    gate_ref[...] = jnp.where(row8 == 0, p1 / denom * g_w, jnp.where(row8 == 1, p2 / denom * g_w, 0.0))


def _merge(x2d, o_a, o_b, gates, w_a_o, w_b_o, w_out, g_ffn, wr_t, hn_all, row_off, tm=256):
    m, d = x2d.shape
    tm = min(tm, m)
    assert row_off % tm == 0
    blk_off = row_off // tm
    wa = o_a.shape[1]
    const = lambda shape: pl.BlockSpec(shape, lambda i: (0, 0), pipeline_mode=pl.Buffered(1))
    return pl.pallas_call(
        _merge_kernel,
        grid=(m // tm,),
        in_specs=[pl.BlockSpec((tm, d), lambda i: (i, 0)),
                  pl.BlockSpec((tm, wa), lambda i: (i, 0)),
                  pl.BlockSpec((tm, wa), lambda i: (i, 0)),
                  pl.BlockSpec((tm, 2 * d), lambda i: (i, 0)),
                  const((wa, d)), const((wa, d)), const((d, d)), const((1, d)), const(wr_t.shape),
                  pl.BlockSpec(memory_space=pl.ANY)],
        out_specs=[pl.BlockSpec((tm, d), lambda i: (i, 0)), pl.BlockSpec((tm, d), lambda i: (i + blk_off, 0)),
                   pl.BlockSpec((8, tm), lambda i: (0, i)), pl.BlockSpec((8, tm), lambda i: (0, i))],
        out_shape=[jax.ShapeDtypeStruct((m, d), F32), jax.ShapeDtypeStruct(hn_all.shape, F32),
                   jax.ShapeDtypeStruct((8, m), I32), jax.ShapeDtypeStruct((8, m), F32)],
        input_output_aliases={9: 1},
        compiler_params=_cparams("parallel"),
        name="merge_router",
    )(x2d, o_a, o_b, gates, w_a_o, w_b_o, w_out, g_ffn.reshape(1, d), wr_t, hn_all)


def _row_token_kernel(dest_ref, rt_ref, *, n_assign, m_tot, n_rows):
    def init(r, carry):
        rt_ref[r] = 0
        return carry

    def scatter(a, carry):
        rt_ref[dest_ref[a]] = jnp.where(a >= m_tot, a - m_tot, a)
        return carry

    lax.fori_loop(0, n_rows, init, 0, unroll=8)
    lax.fori_loop(0, n_assign, scatter, 0, unroll=8)


def _row_tokens(dest, m_tot, n_rows):
    n_assign = dest.shape[0]
    grid_spec = pltpu.PrefetchScalarGridSpec(
        num_scalar_prefetch=1,
        grid=(1,),
        in_specs=[],
        out_specs=pl.BlockSpec(memory_space=pltpu.SMEM),
    )
    return pl.pallas_call(
        functools.partial(_row_token_kernel, n_assign=n_assign, m_tot=m_tot, n_rows=n_rows),
        grid_spec=grid_spec,
        out_shape=jax.ShapeDtypeStruct((n_rows,), I32),
        compiler_params=_cparams("arbitrary"),
        name="moe_row_tokens",
    )(dest)


def _expert_kernel(te_ref, nv_ref, tr_ref, rt_ref, hn_ref, w1_ref, w3_ref, wd_ref, y_ref, xbuf, w1_s, w3_s, wd_s, sems,
                   *, tm, group):
    i = pl.program_id(0)
    n_valid = nv_ref[0]

    def row_copy(tile, slot, r):
        tok = rt_ref[tile * tm + r]
        return pltpu.make_async_copy(hn_ref.at[pl.ds(tok, 1)], xbuf.at[slot, pl.ds(r, 1)], sems.at[slot])

    def for_occupied_rows(tile, fn):
        def body(g, carry):
            for k in range(group):
                fn(g * group + k)
            return carry
        lax.fori_loop(0, (tr_ref[tile] + group - 1) // group, body, 0)

    def start_gather(tile, slot):
        for_occupied_rows(tile, lambda r: row_copy(tile, slot, r).start())

    def wait_gather(tile, slot):
        for_occupied_rows(tile, lambda r: row_copy(tile, slot, r).wait())

    @pl.when(i == 0)
    def _():
        xbuf[...] = jnp.zeros(xbuf.shape, F32)
        start_gather(0, 0)

    @pl.when(i + 1 < n_valid)
    def _():
        start_gather(i + 1, (i + 1) % 2)

    changed = jnp.logical_or(i == 0, te_ref[i] != te_ref[jnp.maximum(i - 1, 0)])

    @pl.when(changed)
    def _():
        w1_s[...] = w1_ref[...].astype(BF16)
        w3_s[...] = w3_ref[...].astype(BF16)
        wd_s[...] = wd_ref[...].astype(BF16)

    @pl.when(i < n_valid)
    def _():
        slot = i % 2
        wait_gather(i, slot)
        x = xbuf[slot].astype(BF16)
        u1 = jnp.dot(x, w1_s[...], preferred_element_type=F32)
        u3 = jnp.dot(x, w3_s[...], preferred_element_type=F32)
        hidden = u1 / (1.0 + jnp.exp(-u1)) * u3
        y_ref[...] = jnp.dot(hidden.astype(BF16), wd_s[...], preferred_element_type=F32)

    @pl.when(i >= n_valid)
    def _():
        y_ref[...] = jnp.zeros(y_ref.shape, F32)


def _experts(tile_expert, n_valid, tile_rows, row_token, hn_all, w_up1, w_up3, w_down, tm):
    n_rows = row_token.shape[0]
    d = hn_all.shape[1]
    f = w_up1.shape[2]
    grid_spec = pltpu.PrefetchScalarGridSpec(
        num_scalar_prefetch=4,
        grid=(n_rows // tm,),
        in_specs=[pl.BlockSpec(memory_space=pl.ANY),
                  pl.BlockSpec((None, d, f), lambda i, te, nv, tr, rt: (te[i], 0, 0)),
                  pl.BlockSpec((None, d, f), lambda i, te, nv, tr, rt: (te[i], 0, 0)),
                  pl.BlockSpec((None, f, d), lambda i, te, nv, tr, rt: (te[i], 0, 0))],
        out_specs=pl.BlockSpec((tm, d), lambda i, te, nv, tr, rt: (i, 0)),
        scratch_shapes=[pltpu.VMEM((2, tm, d), F32),
                        pltpu.VMEM((d, f), BF16), pltpu.VMEM((d, f), BF16), pltpu.VMEM((f, d), BF16),
                        pltpu.SemaphoreType.DMA((2,))],
    )
    return pl.pallas_call(
        functools.partial(_expert_kernel, tm=tm, group=8),
        grid_spec=grid_spec,
        out_shape=jax.ShapeDtypeStruct((n_rows, d), F32),
        compiler_params=_cparams("arbitrary"),
        name="moe_experts",
    )(tile_expert, n_valid, tile_rows, row_token, hn_all, w_up1, w_up3, w_down)


def _combine_kernel(dest_ref, h_ref, gate_ref, gfin_ref, ys_ref, y_ref, buf0, buf1, sem, *, tok_off, m_tot, tm):
    base = tok_off + pl.program_id(0) * tm

    def copies(r):
        return (pltpu.make_async_copy(ys_ref.at[pl.ds(dest_ref[base + r], 1)], buf0.at[pl.ds(r, 1)], sem),
                pltpu.make_async_copy(ys_ref.at[pl.ds(dest_ref[m_tot + base + r], 1)], buf1.at[pl.ds(r, 1)], sem))

    def start(r, c):
        for cp in copies(r):
            cp.start()
        return c

    def wait(r, c):
        for cp in copies(r):
            cp.wait()
        return c

    lax.fori_loop(0, tm, start, 0)
    lax.fori_loop(0, tm, wait, 0)
    out = h_ref[...] + gate_ref[:, 0:1] * buf0[...] + gate_ref[:, 1:2] * buf1[...]
    y = out * lax.rsqrt(jnp.mean(out * out, axis=-1, keepdims=True) + EPS)
    y_ref[...] = y * gfin_ref[...]


def _combine(dest, h, gate_cols, g_final, ys, tok_off, m_tot, tm=128):
    m, d = h.shape
    tm = min(tm, m)
    grid_spec = pltpu.PrefetchScalarGridSpec(
        num_scalar_prefetch=1,
        grid=(m // tm,),
        in_specs=[pl.BlockSpec((tm, d), lambda i, dr: (i, 0)),
                  pl.BlockSpec((tm, 2), lambda i, dr: (i, 0)),
                  pl.BlockSpec((1, d), lambda i, dr: (0, 0)),
                  pl.BlockSpec(memory_space=pl.ANY)],
        out_specs=pl.BlockSpec((tm, d), lambda i, dr: (i, 0)),
        scratch_shapes=[pltpu.VMEM((tm, d), F32), pltpu.VMEM((tm, d), F32), pltpu.SemaphoreType.DMA(())],
    )
    return pl.pallas_call(
        functools.partial(_combine_kernel, tok_off=tok_off, m_tot=m_tot, tm=tm),
        grid_spec=grid_spec,
        out_shape=jax.ShapeDtypeStruct((m, d), F32),
        compiler_params=_cparams("arbitrary"),
        name="moe_combine",
    )(dest, h, gate_cols, g_final.reshape(1, d), ys)


def _moe_plan(eids, tm):
    e_flat = eids.reshape(-1)
    n_assign = e_flat.shape[0]
    onehot = (e_flat[:, None] == jnp.arange(N_EXPERTS, dtype=I32)[None, :]).astype(I32)
    csum = jnp.cumsum(onehot, axis=0)
    rank = jnp.sum((csum - onehot) * onehot, axis=1)
    counts = csum[-1]
    tiles_e = (counts + tm - 1) // tm
    tile_end = jnp.cumsum(tiles_e)
    dest = ((tile_end - tiles_e) * tm)[e_flat] + rank
    n_tiles = n_assign // tm + N_EXPERTS
    n_valid = tile_end[-1]
    tile_ids = jnp.minimum(jnp.arange(n_tiles, dtype=I32), n_valid - 1)
    tile_expert = jnp.sum((tile_end[None, :] <= tile_ids[:, None]).astype(I32), axis=1)
    tile_local = tile_ids - (tile_end - tiles_e)[tile_expert]
    tile_rows = jnp.clip(counts[tile_expert] - tile_local * tm, 0, tm)
    return (dest.astype(I32), tile_expert.astype(I32), n_valid.reshape(1).astype(I32), tile_rows.astype(I32),
            n_tiles)


def kernel(x_prompt, x_sample, cache_k_a, cache_v_a, cache_kidx_a, cache_k_b, cache_v_b, cache_logf_b,
           page_table, g_mix, w_in, f_bias, w_a_o, w_b_o, w_out, g_ffn, w_grp, w_exp, w_up1, w_up3,
           w_down, g_final):
    depth = w_in.shape[0]
    assert depth == 1, "single-layer stack"
    batch, seq, d = x_prompt.shape
    n_dec, dec_seq, _ = x_sample.shape
    assert batch == 1 and dec_seq == 1
    n_pool, page = cache_k_a.shape[1], cache_k_a.shape[2]
    n_pages = page_table.shape[1]
    past = n_pages * page
    l = 0

    pp = _project_group(x_prompt.reshape(seq, d), jnp.arange(seq), g_mix[l], w_in[l], f_bias[l], True)
    ps = _project_group(x_sample.reshape(n_dec, d), jnp.full((n_dec,), past, I32), g_mix[l], w_in[l], f_bias[l], False)
    w_a = pp["q_a"].shape[1]
    h_a = h_b = w_a // HEAD_DIM

    o_a_p = _dsa_prompt(pp["q_a"], pp["k_a"], pp["v_a_t"], pp["q_i"], pp["k_i"], pp["w_i"].T)
    c_parts = _cumsum_parts(pp["log_f"].T)
    kb = jnp.pad(c_parts.reshape(3 * h_b, seq).T, ((0, 0), (0, LANES - 3 * h_b))).astype(BF16)
    o_b_p = _fox_prompt(pp["q_b"], pp["k_b"], kb, pp["v_b_t"])

    paged = lambda c: c[l].reshape(n_pool, page * h_a, HEAD_DIM)
    per_head = lambda a: a.reshape(n_dec, h_a, HEAD_DIM)
    keys_s, self_key = _sample_scores(ps["q_i"], ps["w_i"], ps["k_i"], jnp.swapaxes(cache_kidx_a[l], 1, 2), page_table)
    bias_s, selfb = _sample_select(keys_s, self_key, page, h_a)
    selfb_a = jnp.broadcast_to(selfb[:, 0:1, None], (n_dec, h_a, 1))
    o_a_s = _sample_attn(per_head(ps["q_a"]), per_head(ps["k_a32"]), per_head(ps["v_a32"]),
                         paged(cache_k_a), paged(cache_v_a), page_table, selfb_a,
                         bias_s.reshape(n_dec, n_pages, 1, page * h_a), fox=False).reshape(n_dec, w_a)
    cache_lf_t = jnp.swapaxes(cache_logf_b[l], 1, 2)
    o_b_s = _sample_attn(per_head(ps["q_b"]), per_head(ps["k_b32"]), per_head(ps["v_b32"]),
                         paged(cache_k_b), paged(cache_v_b), page_table, ps["log_f"].reshape(n_dec, h_b, 1),
                         cache_lf_t, fox=True).reshape(n_dec, w_a)

    m_tot = seq + n_dec
    wr_t = jnp.zeros((8 + N_EXPERTS, d), F32).at[:N_GROUPS].set(w_grp[l].T).at[8:].set(w_exp[l].T)
    wao, wbo, wo = w_a_o[l].astype(BF16), w_b_o[l].astype(BF16), w_out[l].astype(BF16)
    hn_all = jnp.zeros((m_tot, d), F32)
    h_s, hn_all, eid_s, gate_s = _merge(x_sample.reshape(n_dec, d), o_a_s, o_b_s, ps["gates"], wao, wbo, wo,
                                        g_ffn[l], wr_t, hn_all, seq)
    h_p, hn_all, eid_p, gate_p = _merge(x_prompt.reshape(seq, d), o_a_p, o_b_p, pp["gates"], wao, wbo, wo,
                                        g_ffn[l], wr_t, hn_all, 0)

    tm_e = 256
    eids = jnp.concatenate([eid_p[:2], eid_s[:2]], axis=1)
    dest, tile_expert, n_valid, tile_rows, n_tiles = _moe_plan(eids, tm_e)
    row_token = _row_tokens(dest, m_tot, n_tiles * tm_e)
    ys = _experts(tile_expert, n_valid, tile_rows, row_token, hn_all, w_up1[l], w_up3[l], w_down[l], tm_e)
    y_p = _combine(dest, h_p, gate_p[:2].T, g_final, ys, 0, m_tot)
    y_s = _combine(dest, h_s, gate_s[:2].T, g_final, ys, seq, m_tot)

    kv = lambda a, b, t, nh: a.reshape(depth, b, t, nh, HEAD_DIM)
    return (y_p.reshape(batch, seq, d), y_s.reshape(n_dec, dec_seq, d),
            kv(pp["k_a32"], batch, seq, h_a), kv(pp["v_a32"], batch, seq, h_a),
            pp["k_i"].reshape(depth, batch, seq, D_IDX),
            kv(pp["k_b32"], batch, seq, h_b), kv(pp["v_b32"], batch, seq, h_b),
            pp["log_f"].reshape(depth, batch, seq, h_b),
            kv(ps["k_a32"], n_dec, dec_seq, h_a), kv(ps["v_a32"], n_dec, dec_seq, h_a),
            ps["k_i"].reshape(depth, n_dec, dec_seq, D_IDX),
            kv(ps["k_b32"], n_dec, dec_seq, h_b), kv(ps["v_b32"], n_dec, dec_seq, h_b),
            ps["log_f"].reshape(depth, n_dec, dec_seq, h_b))
```

```python
import functools

import numpy as np
import jax
import jax.numpy as jnp
from jax import lax
from jax.experimental import pallas as pl
from jax.experimental.pallas import tpu as pltpu

F32 = jnp.float32
BF16 = jnp.bfloat16
I32 = jnp.int32

HEAD_DIM = 128
H_IDX = 16
D_IDX = 64
TOPK_MAX = 256
ROT_A = HEAD_DIM // 4
ROT_IDX = D_IDX // 4
ROPE_THETA = 500000.0
N_GROUPS = 4
EXPERTS_PER_GROUP = 8
N_EXPERTS = N_GROUPS * EXPERTS_PER_GROUP
EPS = 1e-6

LANES = 128
NEG = -1e30
INT_MIN = -(2 ** 31)
VMEM_LIMIT = 56 * 1024 * 1024

NT_DIMS = (((1,), (1,)), ((), ()))


def _cparams(*sem, vmem=VMEM_LIMIT):
    return pltpu.CompilerParams(dimension_semantics=sem, vmem_limit_bytes=vmem)


def _rmsnorm_kernel(x_ref, g_ref, o_ref):
    x = x_ref[...]
    y = x * lax.rsqrt(jnp.mean(x * x, axis=-1, keepdims=True) + EPS)
    o_ref[...] = (y * g_ref[...]).astype(o_ref.dtype)


def _rmsnorm(x, g, out_dtype, tm):
    m, d = x.shape
    return pl.pallas_call(
        _rmsnorm_kernel,
        grid=(m // tm,),
        in_specs=[pl.BlockSpec((tm, d), lambda i: (i, 0)), pl.BlockSpec((1, d), lambda i: (0, 0))],
        out_specs=pl.BlockSpec((tm, d), lambda i: (i, 0)),
        out_shape=jax.ShapeDtypeStruct((m, d), out_dtype),
        compiler_params=_cparams("parallel"),
        name="rmsnorm",
    )(x, g.reshape(1, d))


def _rope_tables(pos, rot_dim, period, active_lanes):
    half = rot_dim // 2
    inv_freq = ROPE_THETA ** (-jnp.arange(half, dtype=F32) / half)
    ang = pos.astype(F32)[:, None] * inv_freq[None, :]
    cos, sin = jnp.cos(ang), jnp.sin(ang)
    lane = np.arange(LANES)
    d = lane % period
    first = (d < half) & (lane < active_lanes)
    second = (d >= half) & (d < rot_dim) & (lane < active_lanes)
    idx = np.where(first, d, np.where(second, d - half, 0))
    cos_l, sin_l = cos[:, idx], sin[:, idx]
    rot = jnp.asarray(first | second)[None, :]
    c = jnp.where(rot, cos_l, 1.0)
    s1 = jnp.where(jnp.asarray(first)[None, :], -sin_l, 0.0)
    s2 = jnp.where(jnp.asarray(second)[None, :], sin_l, 0.0)
    return c, s1, s2


def _rope_lanes(y, c, s1, s2, half):
    return y * c + pltpu.roll(y, LANES - half, 1) * s1 + pltpu.roll(y, half, 1) * s2


def _proj_kernel(*refs, mode, half, n_out, out_scale):
    x_ref, w_ref = refs[0], refs[1]
    outs = refs[len(refs) - n_out:]
    acc = jnp.dot(x_ref[...], w_ref[...], preferred_element_type=F32)
    tn = acc.shape[1]
    if mode == "rope":
        c, s1, s2 = refs[2][...], refs[3][...], refs[4][...]
        for b in range(tn // LANES):
            sl = slice(b * LANES, (b + 1) * LANES)
            y = _rope_lanes(acc[:, sl], c, s1, s2, half)
            if out_scale != 1.0:
                y = y * out_scale
            for o in outs:
                o[:, sl] = y.astype(o.dtype)
        return
    if mode == "sigmoid":
        acc = 1.0 / (1.0 + jnp.exp(-acc))
    if out_scale != 1.0:
        acc = acc * out_scale
    for o in outs:
        o[...] = acc.astype(o.dtype)


def _proj_t_kernel(x_ref, wt_ref, o_ref):
    o_ref[...] = lax.dot_general(wt_ref[...], x_ref[...], NT_DIMS, preferred_element_type=F32).astype(o_ref.dtype)


def _proj_t(xn, w_t, tm=512):
    m, d = xn.shape
    n = w_t.shape[0]
    tm = min(tm, m)
    return pl.pallas_call(
        _proj_t_kernel,
        grid=(m // tm,),
        in_specs=[pl.BlockSpec((tm, d), lambda i: (i, 0)), pl.BlockSpec((n, d), lambda i: (0, 0))],
        out_specs=pl.BlockSpec((n, tm), lambda i: (0, i)),
        out_shape=jax.ShapeDtypeStruct((n, m), BF16),
        compiler_params=_cparams("parallel"),
        name="proj_t",
    )(xn, w_t)


def _proj(xn, w, out_dtypes, mode="none", tables=None, half=0, tm=512, tn=1024, out_scale=1.0):
    m, d = xn.shape
    n = w.shape[1]
    tm = min(tm, m)
    in_specs = [pl.BlockSpec((tm, d), lambda j, i: (i, 0)), pl.BlockSpec((d, tn), lambda j, i: (0, j))]
    args = [xn, w]
    if mode == "rope":
        in_specs += [pl.BlockSpec((tm, LANES), lambda j, i: (i, 0))] * 3
        args += list(tables)
    outs = pl.pallas_call(
        functools.partial(_proj_kernel, mode=mode, half=half, n_out=len(out_dtypes), out_scale=out_scale),
        grid=(n // tn, m // tm),
        in_specs=in_specs,
        out_specs=[pl.BlockSpec((tm, tn), lambda j, i: (i, j)) for _ in out_dtypes],
        out_shape=[jax.ShapeDtypeStruct((m, n), dt) for dt in out_dtypes],
        compiler_params=_cparams("parallel", "parallel"),
        name="proj_" + mode,
    )(*args)
    return outs


def _proj_small_kernel(x_ref, w_ref, c_ref, s1_ref, s2_ref, bias_ref, o_ref, *, half, wi_scale):
    acc = jnp.dot(x_ref[...], w_ref[...], preferred_element_type=F32)
    roped = _rope_lanes(acc, c_ref[...], s1_ref[...], s2_ref[...], half)
    z = acc + bias_ref[...]
    logsig = -(jnp.maximum(-z, 0.0) + jnp.log(1.0 + jnp.exp(-jnp.abs(z))))
    lane = lax.broadcasted_iota(I32, acc.shape, 1)
    o_ref[...] = jnp.where(lane < D_IDX, roped, jnp.where(lane < D_IDX + H_IDX, acc * wi_scale, logsig))


def _proj_small(xn, w_small, tables, bias_row, tm=512):
    m, d = xn.shape
    tm = min(tm, m)
    wi_scale = (H_IDX ** -0.5) * (D_IDX ** -0.5)
    return pl.pallas_call(
        functools.partial(_proj_small_kernel, half=ROT_IDX // 2, wi_scale=wi_scale),
        grid=(m // tm,),
        in_specs=[pl.BlockSpec((tm, d), lambda i: (i, 0)), pl.BlockSpec((d, LANES), lambda i: (0, 0))]
        + [pl.BlockSpec((tm, LANES), lambda i: (i, 0))] * 3 + [pl.BlockSpec((1, LANES), lambda i: (0, 0))],
        out_specs=pl.BlockSpec((tm, LANES), lambda i: (i, 0)),
        out_shape=jax.ShapeDtypeStruct((m, LANES), F32),
        compiler_params=_cparams("parallel"),
        name="proj_small",
    )(xn, w_small, *tables, bias_row)


def _project_group(x2d, pos, g_mix, w_in, f_bias, feature_major_v):
    m, d = x2d.shape
    h_a = h_b = (w_in.shape[1] - (H_IDX * D_IDX + D_IDX + H_IDX) - 2 * d) // (6 * HEAD_DIM + 1)
    w_a = h_a * HEAD_DIM
    sizes = (w_a, w_a, w_a, H_IDX * D_IDX, D_IDX, H_IDX, w_a, w_a, w_a, h_b, d, d)
    offs = np.concatenate([[0], np.cumsum(sizes)])
    seg = lambda k: w_in[:, offs[k]:offs[k + 1]].astype(BF16)

    xn = _rmsnorm(x2d, g_mix, BF16, tm=min(512, m))
    tab_a = _rope_tables(pos, ROT_A, HEAD_DIM, LANES)
    tab_i = _rope_tables(pos, ROT_IDX, D_IDX, LANES)
    tab_k = _rope_tables(pos, ROT_IDX, D_IDX, ROT_IDX)

    qk_scale = HEAD_DIM ** -0.5
    (q_a,) = _proj(xn, seg(0), [BF16], "rope", tab_a, ROT_A // 2, out_scale=qk_scale)
    k_a32, k_a = _proj(xn, seg(1), [F32, BF16], "rope", tab_a, ROT_A // 2)
    (v_a32,) = _proj(xn, seg(2), [F32])
    (q_i,) = _proj(xn, seg(3), [BF16], "rope", tab_i, ROT_IDX // 2)
    (q_b,) = _proj(xn, seg(6), [BF16], out_scale=qk_scale)
    k_b32, k_b = _proj(xn, seg(7), [F32, BF16])
    (v_b32,) = _proj(xn, seg(8), [F32])
    (gates,) = _proj(xn, w_in[:, offs[10]:offs[12]].astype(BF16), [F32], "sigmoid")

    n_small = D_IDX + H_IDX + h_b
    w_small = jnp.concatenate([w_in[:, offs[4]:offs[6]], w_in[:, offs[9]:offs[10]],
                               jnp.zeros((d, LANES - n_small), F32)], axis=1).astype(BF16)
    bias_row = jnp.zeros((1, LANES), F32).at[0, D_IDX + H_IDX:n_small].set(f_bias)
    small = _proj_small(xn, w_small, tab_k, bias_row)
    k_i = small[:, :D_IDX]
    w_i = small[:, D_IDX:D_IDX + H_IDX]
    log_f = small[:, D_IDX + H_IDX:n_small]
    out = dict(q_a=q_a, k_a=k_a, k_a32=k_a32, v_a32=v_a32, q_i=q_i, k_i=k_i, w_i=w_i,
               q_b=q_b, k_b=k_b, k_b32=k_b32, v_b32=v_b32, log_f=log_f, gates=gates)
    if feature_major_v:
        out["v_a_t"] = _proj_t(xn, seg(2).T)
        out["v_b_t"] = _proj_t(xn, seg(8).T)
    return out


def _split3(x):
    hi = x.astype(BF16).astype(F32)
    r1 = x - hi
    mid = r1.astype(BF16).astype(F32)
    lo = (r1 - mid).astype(BF16).astype(F32)
    return hi, mid, lo


def _cumsum_kernel(x_ref, o_ref, *, blk):
    n = x_ref.shape[1]
    r = lax.broadcasted_iota(I32, (blk, blk), 0)
    c = lax.broadcasted_iota(I32, (blk, blk), 1)
    tri = (r <= c).astype(F32)

    def body(i, carry):
        off = pl.multiple_of(i * blk, blk)
        xb = x_ref[:, pl.ds(off, blk)]
        cs = jnp.dot(xb, tri, preferred_element_type=F32, precision=lax.Precision.HIGHEST) + carry
        for j, part in enumerate(_split3(cs)):
            o_ref[j, :, pl.ds(off, blk)] = part
        return cs[:, blk - 1:blk]

    lax.fori_loop(0, n // blk, body, jnp.zeros((x_ref.shape[0], 1), F32))


def _cumsum_parts(x_t, blk=LANES):
    return pl.pallas_call(
        functools.partial(_cumsum_kernel, blk=blk),
        out_shape=jax.ShapeDtypeStruct((3,) + x_t.shape, F32),
        compiler_params=_cparams(),
        name="cumsum_logf",
    )(x_t)


def _causal_steps(nq, tq, tk):
    qi_l, ki_l = [], []
    for qi in range(nq):
        for ki in range((qi * tq + tq - 1) // tk + 1):
            qi_l.append(qi)
            ki_l.append(ki)
    return jnp.asarray(np.array(qi_l, np.int32)), jnp.asarray(np.array(ki_l, np.int32))


def _attn_head_t(kx, qx, vt_h, mask, h, m_ref, l_ref, acc_ref):
    s = lax.dot_general(kx, qx, NT_DIMS, preferred_element_type=F32)
    if mask is not None:
        s = jnp.where(mask, s, NEG)
    m_prev = m_ref[h]
    m_new = jnp.maximum(m_prev, jnp.max(s, axis=0, keepdims=True))
    alpha = jnp.exp(m_prev - m_new)
    p = jnp.exp(s - m_new)
    l_ref[h] = alpha * l_ref[h] + jnp.sum(p, axis=0, keepdims=True)
    acc_ref[h] = alpha * acc_ref[h] + jnp.dot(vt_h, p.astype(BF16), preferred_element_type=F32)
    m_ref[h] = m_new


def _attn_init(m_ref, l_ref, acc_ref):
    m_ref[...] = jnp.full(m_ref.shape, NEG, F32)
    l_ref[...] = jnp.zeros(l_ref.shape, F32)
    acc_ref[...] = jnp.zeros(acc_ref.shape, F32)


def _attn_finish(o_ref, l_ref, acc_ref, nh):
    for h in range(nh):
        o_ref[:, h * HEAD_DIM:(h + 1) * HEAD_DIM] = (acc_ref[h] / l_ref[h]).T.astype(o_ref.dtype)


def _attn_scratch(nh, tq):
    return [pltpu.VMEM((nh, 1, tq), F32), pltpu.VMEM((nh, 1, tq), F32), pltpu.VMEM((nh, HEAD_DIM, tq), F32)]


def _fox_kernel(qi_ref, ki_ref, q_ref, k_ref, kb_ref, vt_ref, ones_ref, o_ref, m_ref, l_ref, acc_ref, *, tq, tk, nh):
    step = pl.program_id(0)
    qi, ki = qi_ref[step], ki_ref[step]

    @pl.when(ki == 0)
    def _():
        _attn_init(m_ref, l_ref, acc_ref)

    def run(masked, n_keys):
        mask = None
        if masked:
            kpos = ki * tk + lax.broadcasted_iota(I32, (n_keys, tq), 0)
            qpos = qi * tq + lax.broadcasted_iota(I32, (n_keys, tq), 1)
            mask = kpos <= qpos
        kb = kb_ref[:n_keys, :]
        for h in range(nh):
            sl = slice(h * HEAD_DIM, (h + 1) * HEAD_DIM)
            kx = jnp.concatenate([k_ref[:n_keys, sl], kb], axis=1)
            qx = jnp.concatenate([q_ref[:, sl], ones_ref[h]], axis=1)
            _attn_head_t(kx, qx, vt_ref[sl, :n_keys], mask, h, m_ref, l_ref, acc_ref)

    below_diagonal = ki * tk + tk - 1 <= qi * tq
    first_half_only = jnp.logical_and(jnp.logical_not(below_diagonal), (qi + 1) * tq <= ki * tk + tk // 2)

    @pl.when(below_diagonal)
    def _():
        run(False, tk)

    @pl.when(first_half_only)
    def _():
        run(True, tk // 2)

    @pl.when(jnp.logical_and(jnp.logical_not(below_diagonal), jnp.logical_not(first_half_only)))
    def _():
        run(True, tk)

    @pl.when(ki == (qi * tq + tq - 1) // tk)
    def _():
        _attn_finish(o_ref, l_ref, acc_ref, nh)


def _fox_prompt(q, k, kb, v_t, tq=512, tk=1024):
    t, w = q.shape
    nh = w // HEAD_DIM
    qi_arr, ki_arr = _causal_steps(t // tq, tq, tk)
    ones = np.zeros((nh, tq, LANES), np.float32)
    for h in range(nh):
        for part in range(3):
            ones[h, :, part * nh + h] = -1.0
    grid_spec = pltpu.PrefetchScalarGridSpec(
        num_scalar_prefetch=2,
        grid=(qi_arr.shape[0],),
        in_specs=[
            pl.BlockSpec((tq, w), lambda s, qi, ki: (qi[s], 0)),
            pl.BlockSpec((tk, w), lambda s, qi, ki: (ki[s], 0)),
            pl.BlockSpec((tk, LANES), lambda s, qi, ki: (ki[s], 0)),
            pl.BlockSpec((w, tk), lambda s, qi, ki: (0, ki[s])),
            pl.BlockSpec((nh, tq, LANES), lambda s, qi, ki: (0, 0, 0)),
        ],
        out_specs=pl.BlockSpec((tq, w), lambda s, qi, ki: (qi[s], 0)),
        scratch_shapes=_attn_scratch(nh, tq),
    )
    return pl.pallas_call(
        functools.partial(_fox_kernel, tq=tq, tk=tk, nh=nh),
        grid_spec=grid_spec,
        out_shape=jax.ShapeDtypeStruct((t, w), BF16),
        compiler_params=_cparams("arbitrary"),
        name="fox_prompt",
    )(qi_arr, ki_arr, q, k, kb, v_t, jnp.asarray(ones, BF16))


def _sortable_key(score):
    bits = lax.bitcast_convert_type(score, I32)
    return jnp.where(bits < 0, bits ^ jnp.int32(0x7FFFFFFF), bits)


def _kth_largest_key(count_ge, shape, topk):
    au = jnp.zeros(shape, I32)
    for bit in range(31, -1, -1):
        bitval = jnp.int32(INT_MIN) if bit == 31 else jnp.int32(1 << bit)
        candu = au | bitval
        cnt = count_ge(candu ^ jnp.int32(INT_MIN))
        au = jnp.where(cnt >= topk, candu, au)
    return jnp.maximum(au ^ jnp.int32(INT_MIN), jnp.int32(INT_MIN + 1))


def _dsa_kernel(qi_ref, ki_ref, qa_ref, ka_ref, vt_ref, qidx_ref, kidx_ref, wit_ref, o_ref,
                keys_ref, thr_ref, qs_ref, m_ref, l_ref, acc_ref, *, tq, tk, tc, nh, topk):
    step = pl.program_id(0)
    qi, ki = qi_ref[step], ki_ref[step]
    last_ki = (qi * tq + tq - 1) // tk

    @pl.when(ki == 0)
    def _index_and_threshold():
        _attn_init(m_ref, l_ref, acc_ref)
        for h in range(H_IDX):
            qs_ref[h * tq:(h + 1) * tq, :] = qidx_ref[:, h * D_IDX:(h + 1) * D_IDX]
        n_causal = ((qi + 1) * tq) // tc
        n_chunks = (last_ki + 1) * (tk // tc)
        qpos = qi * tq + lax.broadcasted_iota(I32, (tc, tq), 1)

        def score_chunk(c, carry):
            off = pl.multiple_of(c * tc, tc)
            s = lax.dot_general(kidx_ref[pl.ds(off, tc), :], qs_ref[...], NT_DIMS, preferred_element_type=F32)
            score = jnp.zeros((tc, tq), F32)
            for h in range(H_IDX):
                score = score + wit_ref[h:h + 1, :] * jnp.maximum(s[:, h * tq:(h + 1) * tq], 0.0)
            kpos = off + lax.broadcasted_iota(I32, (tc, tq), 0)
            keys_ref[pl.ds(off, tc), :] = jnp.where(kpos <= qpos, _sortable_key(score), jnp.int32(INT_MIN))
            return carry

        def mask_chunk(c, carry):
            keys_ref[pl.ds(pl.multiple_of(c * tc, tc), tc), :] = jnp.full((tc, tq), INT_MIN, I32)
            return carry

        lax.fori_loop(0, n_causal, score_chunk, 0)
        lax.fori_loop(n_causal, n_chunks, mask_chunk, 0)

        def count_ge(cand):
            def body(g, acc):
                for c in range(tq // tc):
                    off = pl.multiple_of(g * tq + c * tc, tc)
                    ge = (keys_ref[pl.ds(off, tc), :] >= cand).astype(I32)
                    for b in range(tc // 8):
                        acc = acc + ge[b * 8:(b + 1) * 8, :]
                return acc
            acc = lax.fori_loop(0, qi + 1, body, jnp.zeros((8, tq), I32))
            return jnp.sum(acc, axis=0, keepdims=True)

        thr_ref[...] = _kth_largest_key(count_ge, (1, tq), topk)

    def attend(n_keys):
        off_k = pl.multiple_of(ki * tk, tk)
        selected = keys_ref[pl.ds(off_k, n_keys), :] >= thr_ref[...]
        for h in range(nh):
            sl = slice(h * HEAD_DIM, (h + 1) * HEAD_DIM)
            _attn_head_t(ka_ref[:n_keys, sl], qa_ref[:, sl], vt_ref[sl, :n_keys], selected, h, m_ref, l_ref, acc_ref)

    first_half_only = (qi + 1) * tq <= ki * tk + tk // 2

    @pl.when(first_half_only)
    def _():
        attend(tk // 2)

    @pl.when(jnp.logical_not(first_half_only))
    def _():
        attend(tk)

    @pl.when(ki == last_ki)
    def _():
        _attn_finish(o_ref, l_ref, acc_ref, nh)


def _dsa_prompt(q_a, k_a, v_a_t, q_i, k_i, w_i_t, tq=512, tk=1024, tc=128):
    t, w = q_a.shape
    nh = w // HEAD_DIM
    topk = min(TOPK_MAX, t // 4)
    qi_arr, ki_arr = _causal_steps(t // tq, tq, tk)
    grid_spec = pltpu.PrefetchScalarGridSpec(
        num_scalar_prefetch=2,
        grid=(qi_arr.shape[0],),
        in_specs=[
            pl.BlockSpec((tq, w), lambda s, qi, ki: (qi[s], 0)),
            pl.BlockSpec((tk, w), lambda s, qi, ki: (ki[s], 0)),
            pl.BlockSpec((w, tk), lambda s, qi, ki: (0, ki[s])),
            pl.BlockSpec((tq, H_IDX * D_IDX), lambda s, qi, ki: (qi[s], 0)),
            pl.BlockSpec((t, D_IDX), lambda s, qi, ki: (0, 0)),
            pl.BlockSpec((H_IDX, tq), lambda s, qi, ki: (0, qi[s])),
        ],
        out_specs=pl.BlockSpec((tq, w), lambda s, qi, ki: (qi[s], 0)),
        scratch_shapes=[
            pltpu.VMEM((t, tq), I32),
            pltpu.VMEM((1, tq), I32),
            pltpu.VMEM((H_IDX * tq, D_IDX), BF16),
        ] + _attn_scratch(nh, tq),
    )
    return pl.pallas_call(
        functools.partial(_dsa_kernel, tq=tq, tk=tk, tc=tc, nh=nh, topk=topk),
        grid_spec=grid_spec,
        out_shape=jax.ShapeDtypeStruct((t, w), BF16),
        compiler_params=_cparams("arbitrary"),
        name="dsa_prompt",
    )(qi_arr, ki_arr, q_a, k_a, v_a_t, q_i, k_i.astype(BF16), w_i_t)


def _sample_score_kernel(pt_ref, q_ref, w_ref, knew_ref, *rest, n_pages, page):
    kpages = rest[:n_pages]
    key_ref, self_ref = rest[n_pages], rest[n_pages + 1]
    q = q_ref[...]
    w = w_ref[...]

    def key_of(s):
        return _sortable_key(jnp.sum(w * jnp.maximum(s, 0.0), axis=0, keepdims=True))

    for j in range(n_pages):
        s = jnp.dot(q, kpages[j][...].astype(BF16), preferred_element_type=F32)
        key_ref[:, j * page:(j + 1) * page] = key_of(s)
    knew = jnp.broadcast_to(knew_ref[...].astype(BF16), (8, D_IDX))
    s_new = lax.dot_general(q, knew, NT_DIMS, preferred_element_type=F32)
    self_ref[...] = jnp.broadcast_to(key_of(s_new)[:, 0:1], self_ref.shape)


def _sample_scores(q_i, w_i, k_i_new, cache_kidx_t, page_table):
    n, n_pages = page_table.shape
    page = cache_kidx_t.shape[2]
    past = n_pages * page
    pt_flat = page_table.reshape(-1)
    kspecs = [pl.BlockSpec((None, D_IDX, page), functools.partial(lambda i, pt, j: (pt[i * n_pages + j], 0, 0), j=j))
              for j in range(n_pages)]
    grid_spec = pltpu.PrefetchScalarGridSpec(
        num_scalar_prefetch=1,
        grid=(n,),
        in_specs=[pl.BlockSpec((None, H_IDX, D_IDX), lambda i, pt: (i, 0, 0)),
                  pl.BlockSpec((None, H_IDX, 1), lambda i, pt: (i, 0, 0)),
                  pl.BlockSpec((None, 1, D_IDX), lambda i, pt: (i, 0, 0))] + kspecs,
        out_specs=[pl.BlockSpec((None, 1, past), lambda i, pt: (i, 0, 0)),
                   pl.BlockSpec((None, 1, LANES), lambda i, pt: (i, 0, 0))],
    )
    keys, self_key = pl.pallas_call(
        functools.partial(_sample_score_kernel, n_pages=n_pages, page=page),
        grid_spec=grid_spec,
        out_shape=[jax.ShapeDtypeStruct((n, 1, past), I32), jax.ShapeDtypeStruct((n, 1, LANES), I32)],
        compiler_params=_cparams("arbitrary"),
        name="sample_scores",
    )(pt_flat, q_i.reshape(n, H_IDX, D_IDX), w_i.reshape(n, H_IDX, 1), k_i_new.reshape(n, 1, D_IDX),
      *([cache_kidx_t] * n_pages))
    return keys.reshape(n, past), self_key.reshape(n, LANES)


def _sample_select_kernel(key_ref, self_ref, expand_ref, bias_ref, selfb_ref, *, n_pages, page, nh, topk):
    keys = key_ref[...]
    self_key = self_ref[:, 0:1]

    def count_ge(cand):
        past = jnp.sum((keys >= cand).astype(I32), axis=1, keepdims=True)
        return past + (self_key >= cand).astype(I32)

    thr = _kth_largest_key(count_ge, self_key.shape, topk)
    expand = expand_ref[...]
    rows = page * nh
    for j in range(n_pages):
        sel = jnp.where(keys[:, j * page:(j + 1) * page] >= thr, 1.0, 0.0).astype(BF16)
        sel_x = jnp.dot(sel, expand, preferred_element_type=F32)
        bias_ref[:, j * rows:(j + 1) * rows] = (1.0 - sel_x) * NEG
    selfb_ref[...] = jnp.broadcast_to(jnp.where(self_key >= thr, 0.0, NEG), selfb_ref.shape)


def _sample_select(keys, self_key, page, nh):
    n, past = keys.shape
    n_pages = past // page
    topk = min(TOPK_MAX, (past + 1) // 4)
    expand = np.zeros((page, page * nh), np.float32)
    for k in range(page):
        expand[k, k * nh:(k + 1) * nh] = 1.0
    return pl.pallas_call(
        functools.partial(_sample_select_kernel, n_pages=n_pages, page=page, nh=nh, topk=topk),
        out_shape=[jax.ShapeDtypeStruct((n, past * nh), F32), jax.ShapeDtypeStruct((n, LANES), F32)],
        compiler_params=_cparams(),
        name="sample_select",
    )(keys, self_key, jnp.asarray(expand, BF16))


def _sample_attn_kernel(pt_ref, q_ref, knew_ref, vnew_ref, selfb_ref, *rest, pps, nh, fox):
    kp = rest[:pps]
    vp = rest[pps:2 * pps]
    if fox:
        lfp, pre_ref = rest[2 * pps:3 * pps], rest[3 * pps]
        o_ref, m_ref, l_ref, acc_ref, carry_ref = rest[3 * pps + 1:]
    else:
        bias_ref = rest[2 * pps]
        o_ref, m_ref, l_ref, acc_ref = rest[2 * pps + 1:]
    p_step = pl.program_id(1)
    rows = kp[0].shape[0]
    sub = lax.broadcasted_iota(I32, (nh, rows), 0)
    lane = lax.broadcasted_iota(I32, (nh, rows), 1)
    own_head = lax.rem(lane, nh) == sub
    q = q_ref[...]

    @pl.when(p_step == 0)
    def _():
        m_ref[...] = jnp.full(m_ref.shape, NEG, F32)
        l_ref[...] = jnp.zeros(l_ref.shape, F32)
        acc_ref[...] = jnp.zeros(acc_ref.shape, F32)
        if fox:
            carry_ref[...] = jnp.zeros(carry_ref.shape, F32)

    scores = []
    if fox:
        lfs = [lfp[j][...] for j in range(pps)]
        stacked = jnp.concatenate([part for lf in lfs for part in _split3(lf)], axis=0).astype(BF16)
        pre_all = jnp.dot(stacked, pre_ref[...], preferred_element_type=F32)
        carry = carry_ref[...]
    for j in range(pps):
        s = lax.dot_general(q, kp[j][...].astype(BF16), NT_DIMS, preferred_element_type=F32)
        if fox:
            pre = carry
            for part in range(3):
                pre = pre + pre_all[(3 * j + part) * nh:(3 * j + part + 1) * nh]
            carry = carry + jnp.sum(lfs[j], axis=1, keepdims=True)
            s = s - pre
        else:
            s = s + bias_ref[j]
        scores.append(jnp.where(own_head, s, NEG))
    if fox:
        carry_ref[...] = carry

    m_prev = m_ref[...]
    m_new = m_prev
    for s in scores:
        m_new = jnp.maximum(m_new, jnp.max(s, axis=1, keepdims=True))
    alpha = jnp.exp(m_prev - m_new)
    l_new = alpha * l_ref[...]
    acc = alpha * acc_ref[...]
    for j, s in enumerate(scores):
        p = jnp.exp(s - m_new)
        l_new = l_new + jnp.sum(p, axis=1, keepdims=True)
        acc = acc + jnp.dot(p.astype(BF16), vp[j][...].astype(BF16), preferred_element_type=F32)
    m_ref[...] = m_new
    l_ref[...] = l_new
    acc_ref[...] = acc

    @pl.when(p_step == pl.num_programs(1) - 1)
    def _():
        knew = knew_ref[...].astype(BF16).astype(F32)
        s_self = jnp.sum(q.astype(F32) * knew, axis=1, keepdims=True)
        if fox:
            s_self = s_self - (carry_ref[...] + selfb_ref[...])
        else:
            s_self = s_self + selfb_ref[...]
        m_prev = m_ref[...]
        m_new = jnp.maximum(m_prev, s_self)
        alpha = jnp.exp(m_prev - m_new)
        p = jnp.exp(s_self - m_new)
        l_fin = alpha * l_ref[...] + p
        vnew = vnew_ref[...].astype(BF16).astype(F32)
        acc = alpha * acc_ref[...] + p.astype(BF16).astype(F32) * vnew
        o_ref[...] = (acc / l_fin).astype(o_ref.dtype)


def _sample_attn(q, k_new, v_new, cache_k, cache_v, page_table, selfb, extra, fox, pps=8):
    n, nh, hd = q.shape
    n_pages = page_table.shape[1]
    rows = cache_k.shape[1]
    page = rows // nh
    pt_flat = page_table.reshape(-1)

    pps = min(pps, n_pages)

    def page_spec(j, shape):
        return pl.BlockSpec((None,) + shape, lambda i, p, pt: (pt[i * n_pages + p * pps + j], 0, 0))

    per_seq = lambda shape: pl.BlockSpec((None,) + shape, lambda i, p, pt: (i, 0, 0))
    in_specs = [per_seq((nh, hd))] * 3 + [per_seq((nh, 1))]
    in_specs += [page_spec(j, (rows, hd)) for j in range(pps)] * 2
    args = [q, k_new, v_new, selfb] + [cache_k] * pps + [cache_v] * pps
    scratch = [pltpu.VMEM((nh, 1), F32), pltpu.VMEM((nh, 1), F32), pltpu.VMEM((nh, hd), F32)]
    if fox:
        cache_lf_t = extra
        prefix_expand = np.zeros((page, rows), np.float32)
        for k in range(page):
            prefix_expand[k, k * nh:] = 1.0
        in_specs += [page_spec(j, (nh, page)) for j in range(pps)]
        in_specs += [pl.BlockSpec((page, rows), lambda i, p, pt: (0, 0))]
        args += [cache_lf_t] * pps + [jnp.asarray(prefix_expand, BF16)]
        scratch += [pltpu.VMEM((nh, 1), F32)]
    else:
        bias = extra
        in_specs += [pl.BlockSpec((None, pps, 1, rows), lambda i, p, pt: (i, p, 0, 0))]
        args += [bias]
    grid_spec = pltpu.PrefetchScalarGridSpec(
        num_scalar_prefetch=1,
        grid=(n, n_pages // pps),
        in_specs=in_specs,
        out_specs=per_seq((nh, hd)),
        scratch_shapes=scratch,
    )
    return pl.pallas_call(
        functools.partial(_sample_attn_kernel, pps=pps, nh=nh, fox=fox),
        grid_spec=grid_spec,
        out_shape=jax.ShapeDtypeStruct((n, nh, hd), BF16),
        compiler_params=_cparams("arbitrary", "arbitrary"),
        name="sample_fox" if fox else "sample_dsa",
    )(pt_flat, *args)


def _merge_kernel(x_ref, oa_ref, ob_ref, g_ref, wao_ref, wbo_ref, wout_ref, gffn_ref, wr_ref, hn_in_ref,
                  h_ref, hn_ref, eid_ref, gate_ref):
    del hn_in_ref
    d = x_ref.shape[1]
    br_a = jnp.dot(oa_ref[...], wao_ref[...], preferred_element_type=F32)
    br_b = jnp.dot(ob_ref[...], wbo_ref[...], preferred_element_type=F32)
    mix = g_ref[:, :d] * br_a + g_ref[:, d:] * br_b
    h = x_ref[...] + jnp.dot(mix.astype(BF16), wout_ref[...], preferred_element_type=F32)
    h_ref[...] = h
    hn = h * lax.rsqrt(jnp.mean(h * h, axis=-1, keepdims=True) + EPS) * gffn_ref[...]
    hn_ref[...] = hn

    logits = lax.dot_general(wr_ref[...], hn, NT_DIMS, preferred_element_type=F32,
                             precision=lax.Precision.HIGHEST)
    tm = logits.shape[1]
    gl = logits[0:N_GROUPS]
    row_g = lax.broadcasted_iota(I32, gl.shape, 0)
    gmax = jnp.max(gl, axis=0, keepdims=True)
    g_idx = jnp.min(jnp.where(gl == gmax, row_g, N_GROUPS), axis=0, keepdims=True)
    g_w = 1.0 / jnp.sum(jnp.exp(gl - gmax), axis=0, keepdims=True)
    in_l = jnp.zeros((EXPERTS_PER_GROUP, tm), F32)
    for g in range(N_GROUPS):
        lo = 8 + g * EXPERTS_PER_GROUP
        in_l = in_l + jnp.where(g_idx == g, logits[lo:lo + EXPERTS_PER_GROUP], 0.0)
    e = jnp.exp(in_l - jnp.max(in_l, axis=0, keepdims=True))
    in_p = e / jnp.sum(e, axis=0, keepdims=True)
    row_e = lax.broadcasted_iota(I32, in_p.shape, 0)
    p1 = jnp.max(in_p, axis=0, keepdims=True)
    i1 = jnp.min(jnp.where(in_p == p1, row_e, EXPERTS_PER_GROUP), axis=0, keepdims=True)
    rest = jnp.where(row_e == i1, -1.0, in_p)
    p2 = jnp.max(rest, axis=0, keepdims=True)
    i2 = jnp.min(jnp.where(rest == p2, row_e, EXPERTS_PER_GROUP), axis=0, keepdims=True)
    denom = p1 + p2
    row8 = lax.broadcasted_iota(I32, (8, tm), 0)
    eid_ref[...] = jnp.where(row8 == 0, g_idx * EXPERTS_PER_GROUP + i1,
                             jnp.where(row8 == 1, g_idx * EXPERTS_PER_GROUP + i2, 0))
    gate_ref[...] = jnp.where(row8 == 0, p1 / denom * g_w, jnp.where(row8 == 1, p2 / denom * g_w, 0.0))


def _merge(x2d, o_a, o_b, gates, w_a_o, w_b_o, w_out, g_ffn, wr_t, hn_all, row_off, tm=256):
    m, d = x2d.shape
    tm = min(tm, m)
    assert row_off % tm == 0
    blk_off = row_off // tm
    wa = o_a.shape[1]
    const = lambda shape: pl.BlockSpec(shape, lambda i: (0, 0), pipeline_mode=pl.Buffered(1))
    return pl.pallas_call(
        _merge_kernel,
        grid=(m // tm,),
        in_specs=[pl.BlockSpec((tm, d), lambda i: (i, 0)),
                  pl.BlockSpec((tm, wa), lambda i: (i, 0)),
                  pl.BlockSpec((tm, wa), lambda i: (i, 0)),
                  pl.BlockSpec((tm, 2 * d), lambda i: (i, 0)),
                  const((wa, d)), const((wa, d)), const((d, d)), const((1, d)), const(wr_t.shape),
                  pl.BlockSpec(memory_space=pl.ANY)],
        out_specs=[pl.BlockSpec((tm, d), lambda i: (i, 0)), pl.BlockSpec((tm, d), lambda i: (i + blk_off, 0)),
                   pl.BlockSpec((8, tm), lambda i: (0, i)), pl.BlockSpec((8, tm), lambda i: (0, i))],
        out_shape=[jax.ShapeDtypeStruct((m, d), F32), jax.ShapeDtypeStruct(hn_all.shape, F32),
                   jax.ShapeDtypeStruct((8, m), I32), jax.ShapeDtypeStruct((8, m), F32)],
        input_output_aliases={9: 1},
        compiler_params=_cparams("parallel"),
        name="merge_router",
    )(x2d, o_a, o_b, gates, w_a_o, w_b_o, w_out, g_ffn.reshape(1, d), wr_t, hn_all)


def _row_token_kernel(dest_ref, rt_ref, *, n_assign, m_tot, n_rows):
    def init(r, carry):
        rt_ref[r] = 0
        return carry

    def scatter(a, carry):
        rt_ref[dest_ref[a]] = jnp.where(a >= m_tot, a - m_tot, a)
        return carry

    lax.fori_loop(0, n_rows, init, 0, unroll=8)
    lax.fori_loop(0, n_assign, scatter, 0, unroll=8)


def _row_tokens(dest, m_tot, n_rows):
    n_assign = dest.shape[0]
    grid_spec = pltpu.PrefetchScalarGridSpec(
        num_scalar_prefetch=1,
        grid=(1,),
        in_specs=[],
        out_specs=pl.BlockSpec(memory_space=pltpu.SMEM),
    )
    return pl.pallas_call(
        functools.partial(_row_token_kernel, n_assign=n_assign, m_tot=m_tot, n_rows=n_rows),
        grid_spec=grid_spec,
        out_shape=jax.ShapeDtypeStruct((n_rows,), I32),
        compiler_params=_cparams("arbitrary"),
        name="moe_row_tokens",
    )(dest)


def _expert_kernel(te_ref, nv_ref, tr_ref, rt_ref, hn_ref, w1_ref, w3_ref, wd_ref, y_ref, xbuf, w1_s, w3_s, wd_s, sems,
                   *, tm, group):
    i = pl.program_id(0)
    n_valid = nv_ref[0]

    def row_copy(tile, slot, r):
        tok = rt_ref[tile * tm + r]
        return pltpu.make_async_copy(hn_ref.at[pl.ds(tok, 1)], xbuf.at[slot, pl.ds(r, 1)], sems.at[slot])

    def for_occupied_rows(tile, fn):
        def body(g, carry):
            for k in range(group):
                fn(g * group + k)
            return carry
        lax.fori_loop(0, (tr_ref[tile] + group - 1) // group, body, 0)

    def start_gather(tile, slot):
        for_occupied_rows(tile, lambda r: row_copy(tile, slot, r).start())

    def wait_gather(tile, slot):
        for_occupied_rows(tile, lambda r: row_copy(tile, slot, r).wait())

    @pl.when(i == 0)
    def _():
        xbuf[...] = jnp.zeros(xbuf.shape, F32)
        start_gather(0, 0)

    @pl.when(i + 1 < n_valid)
    def _():
        start_gather(i + 1, (i + 1) % 2)

    changed = jnp.logical_or(i == 0, te_ref[i] != te_ref[jnp.maximum(i - 1, 0)])

    @pl.when(changed)
    def _():
        w1_s[...] = w1_ref[...].astype(BF16)
        w3_s[...] = w3_ref[...].astype(BF16)
        wd_s[...] = wd_ref[...].astype(BF16)

    @pl.when(i < n_valid)
    def _():
        slot = i % 2
        wait_gather(i, slot)
        x = xbuf[slot].astype(BF16)
        u1 = jnp.dot(x, w1_s[...], preferred_element_type=F32)
        u3 = jnp.dot(x, w3_s[...], preferred_element_type=F32)
        hidden = u1 / (1.0 + jnp.exp(-u1)) * u3
        y_ref[...] = jnp.dot(hidden.astype(BF16), wd_s[...], preferred_element_type=F32)

    @pl.when(i >= n_valid)
    def _():
        y_ref[...] = jnp.zeros(y_ref.shape, F32)


def _experts(tile_expert, n_valid, tile_rows, row_token, hn_all, w_up1, w_up3, w_down, tm):
    n_rows = row_token.shape[0]
    d = hn_all.shape[1]
    f = w_up1.shape[2]
    grid_spec = pltpu.PrefetchScalarGridSpec(
        num_scalar_prefetch=4,
        grid=(n_rows // tm,),
        in_specs=[pl.BlockSpec(memory_space=pl.ANY),
                  pl.BlockSpec((None, d, f), lambda i, te, nv, tr, rt: (te[i], 0, 0)),
                  pl.BlockSpec((None, d, f), lambda i, te, nv, tr, rt: (te[i], 0, 0)),
                  pl.BlockSpec((None, f, d), lambda i, te, nv, tr, rt: (te[i], 0, 0))],
        out_specs=pl.BlockSpec((tm, d), lambda i, te, nv, tr, rt: (i, 0)),
        scratch_shapes=[pltpu.VMEM((2, tm, d), F32),
                        pltpu.VMEM((d, f), BF16), pltpu.VMEM((d, f), BF16), pltpu.VMEM((f, d), BF16),
                        pltpu.SemaphoreType.DMA((2,))],
    )
    return pl.pallas_call(
        functools.partial(_expert_kernel, tm=tm, group=8),
        grid_spec=grid_spec,
        out_shape=jax.ShapeDtypeStruct((n_rows, d), F32),
        compiler_params=_cparams("arbitrary"),
        name="moe_experts",
    )(tile_expert, n_valid, tile_rows, row_token, hn_all, w_up1, w_up3, w_down)


def _combine_kernel(dest_ref, h_ref, gate_ref, gfin_ref, ys_ref, y_ref, buf0, buf1, sem, *, tok_off, m_tot, tm):
    base = tok_off + pl.program_id(0) * tm

    def copies(r):
        return (pltpu.make_async_copy(ys_ref.at[pl.ds(dest_ref[base + r], 1)], buf0.at[pl.ds(r, 1)], sem),
                pltpu.make_async_copy(ys_ref.at[pl.ds(dest_ref[m_tot + base + r], 1)], buf1.at[pl.ds(r, 1)], sem))

    def start(r, c):
        for cp in copies(r):
            cp.start()
        return c

    def wait(r, c):
        for cp in copies(r):
            cp.wait()
        return c

    lax.fori_loop(0, tm, start, 0)
    lax.fori_loop(0, tm, wait, 0)
    out = h_ref[...] + gate_ref[:, 0:1] * buf0[...] + gate_ref[:, 1:2] * buf1[...]
    y = out * lax.rsqrt(jnp.mean(out * out, axis=-1, keepdims=True) + EPS)
    y_ref[...] = y * gfin_ref[...]


def _combine(dest, h, gate_cols, g_final, ys, tok_off, m_tot, tm=128):
    m, d = h.shape
    tm = min(tm, m)
    grid_spec = pltpu.PrefetchScalarGridSpec(
        num_scalar_prefetch=1,
        grid=(m // tm,),
        in_specs=[pl.BlockSpec((tm, d), lambda i, dr: (i, 0)),
                  pl.BlockSpec((tm, 2), lambda i, dr: (i, 0)),
                  pl.BlockSpec((1, d), lambda i, dr: (0, 0)),
                  pl.BlockSpec(memory_space=pl.ANY)],
        out_specs=pl.BlockSpec((tm, d), lambda i, dr: (i, 0)),
        scratch_shapes=[pltpu.VMEM((tm, d), F32), pltpu.VMEM((tm, d), F32), pltpu.SemaphoreType.DMA(())],
    )
    return pl.pallas_call(
        functools.partial(_combine_kernel, tok_off=tok_off, m_tot=m_tot, tm=tm),
        grid_spec=grid_spec,
        out_shape=jax.ShapeDtypeStruct((m, d), F32),
        compiler_params=_cparams("arbitrary"),
        name="moe_combine",
    )(dest, h, gate_cols, g_final.reshape(1, d), ys)


def _moe_plan(eids, tm):
    e_flat = eids.reshape(-1)
    n_assign = e_flat.shape[0]
    onehot = (e_flat[:, None] == jnp.arange(N_EXPERTS, dtype=I32)[None, :]).astype(I32)
    csum = jnp.cumsum(onehot, axis=0)
    rank = jnp.sum((csum - onehot) * onehot, axis=1)
    counts = csum[-1]
    tiles_e = (counts + tm - 1) // tm
    tile_end = jnp.cumsum(tiles_e)
    dest = ((tile_end - tiles_e) * tm)[e_flat] + rank
    n_tiles = n_assign // tm + N_EXPERTS
    n_valid = tile_end[-1]
    tile_ids = jnp.minimum(jnp.arange(n_tiles, dtype=I32), n_valid - 1)
    tile_expert = jnp.sum((tile_end[None, :] <= tile_ids[:, None]).astype(I32), axis=1)
    tile_local = tile_ids - (tile_end - tiles_e)[tile_expert]
    tile_rows = jnp.clip(counts[tile_expert] - tile_local * tm, 0, tm)
    return (dest.astype(I32), tile_expert.astype(I32), n_valid.reshape(1).astype(I32), tile_rows.astype(I32),
            n_tiles)


def kernel(x_prompt, x_sample, cache_k_a, cache_v_a, cache_kidx_a, cache_k_b, cache_v_b, cache_logf_b,
           page_table, g_mix, w_in, f_bias, w_a_o, w_b_o, w_out, g_ffn, w_grp, w_exp, w_up1, w_up3,
           w_down, g_final):
    depth = w_in.shape[0]
    assert depth == 1, "single-layer stack"
    batch, seq, d = x_prompt.shape
    n_dec, dec_seq, _ = x_sample.shape
    assert batch == 1 and dec_seq == 1
    n_pool, page = cache_k_a.shape[1], cache_k_a.shape[2]
    n_pages = page_table.shape[1]
    past = n_pages * page
    l = 0

    pp = _project_group(x_prompt.reshape(seq, d), jnp.arange(seq), g_mix[l], w_in[l], f_bias[l], True)
    ps = _project_group(x_sample.reshape(n_dec, d), jnp.full((n_dec,), past, I32), g_mix[l], w_in[l], f_bias[l], False)
    w_a = pp["q_a"].shape[1]
    h_a = h_b = w_a // HEAD_DIM

    o_a_p = _dsa_prompt(pp["q_a"], pp["k_a"], pp["v_a_t"], pp["q_i"], pp["k_i"], pp["w_i"].T)
    c_parts = _cumsum_parts(pp["log_f"].T)
    kb = jnp.pad(c_parts.reshape(3 * h_b, seq).T, ((0, 0), (0, LANES - 3 * h_b))).astype(BF16)
    o_b_p = _fox_prompt(pp["q_b"], pp["k_b"], kb, pp["v_b_t"])

    paged = lambda c: c[l].reshape(n_pool, page * h_a, HEAD_DIM)
    per_head = lambda a: a.reshape(n_dec, h_a, HEAD_DIM)
    keys_s, self_key = _sample_scores(ps["q_i"], ps["w_i"], ps["k_i"], jnp.swapaxes(cache_kidx_a[l], 1, 2), page_table)
    bias_s, selfb = _sample_select(keys_s, self_key, page, h_a)
    selfb_a = jnp.broadcast_to(selfb[:, 0:1, None], (n_dec, h_a, 1))
    o_a_s = _sample_attn(per_head(ps["q_a"]), per_head(ps["k_a32"]), per_head(ps["v_a32"]),
                         paged(cache_k_a), paged(cache_v_a), page_table, selfb_a,
                         bias_s.reshape(n_dec, n_pages, 1, page * h_a), fox=False).reshape(n_dec, w_a)
    cache_lf_t = jnp.swapaxes(cache_logf_b[l], 1, 2)
    o_b_s = _sample_attn(per_head(ps["q_b"]), per_head(ps["k_b32"]), per_head(ps["v_b32"]),
                         paged(cache_k_b), paged(cache_v_b), page_table, ps["log_f"].reshape(n_dec, h_b, 1),
                         cache_lf_t, fox=True).reshape(n_dec, w_a)

    m_tot = seq + n_dec
    wr_t = jnp.zeros((8 + N_EXPERTS, d), F32).at[:N_GROUPS].set(w_grp[l].T).at[8:].set(w_exp[l].T)
    wao, wbo, wo = w_a_o[l].astype(BF16), w_b_o[l].astype(BF16), w_out[l].astype(BF16)
    hn_all = jnp.zeros((m_tot, d), F32)
    h_s, hn_all, eid_s, gate_s = _merge(x_sample.reshape(n_dec, d), o_a_s, o_b_s, ps["gates"], wao, wbo, wo,
                                        g_ffn[l], wr_t, hn_all, seq)
    h_p, hn_all, eid_p, gate_p = _merge(x_prompt.reshape(seq, d), o_a_p, o_b_p, pp["gates"], wao, wbo, wo,
                                        g_ffn[l], wr_t, hn_all, 0)

    tm_e = 256
    eids = jnp.concatenate([eid_p[:2], eid_s[:2]], axis=1)
    dest, tile_expert, n_valid, tile_rows, n_tiles = _moe_plan(eids, tm_e)
    row_token = _row_tokens(dest, m_tot, n_tiles * tm_e)
    ys = _experts(tile_expert, n_valid, tile_rows, row_token, hn_all, w_up1[l], w_up3[l], w_down[l], tm_e)
    y_p = _combine(dest, h_p, gate_p[:2].T, g_final, ys, 0, m_tot)
    y_s = _combine(dest, h_s, gate_s[:2].T, g_final, ys, seq, m_tot)

    kv = lambda a, b, t, nh: a.reshape(depth, b, t, nh, HEAD_DIM)
    return (y_p.reshape(batch, seq, d), y_s.reshape(n_dec, dec_seq, d),
            kv(pp["k_a32"], batch, seq, h_a), kv(pp["v_a32"], batch, seq, h_a),
            pp["k_i"].reshape(depth, batch, seq, D_IDX),
            kv(pp["k_b32"], batch, seq, h_b), kv(pp["v_b32"], batch, seq, h_b),
            pp["log_f"].reshape(depth, batch, seq, h_b),
            kv(ps["k_a32"], n_dec, dec_seq, h_a), kv(ps["v_a32"], n_dec, dec_seq, h_a),
            ps["k_i"].reshape(depth, n_dec, dec_seq, D_IDX),
            kv(ps["k_b32"], n_dec, dec_seq, h_b), kv(ps["v_b32"], n_dec, dec_seq, h_b),
            ps["log_f"].reshape(depth, n_dec, dec_seq, h_b))
```
